```python
import functools
import jax, jax.numpy as jnp
from jax import lax
import numpy as np

D_MODEL = 1024
BATCH = 8
SEQ = 2048
DEPTH = 4
DEC_BATCH = 128
DEC_SEQ = 8
PAST_LEN = 2048
PAGE_SIZE = 128

N_HEADS_A = 8
HEAD_DIM_A = 64
WIDTH_A = N_HEADS_A * HEAD_DIM_A
N_IDX_HEADS = 8
IDX_DIM = 64
TOPK_MAX = 256
INDEX_WEIGHT_SCALE = N_IDX_HEADS ** -0.5 * IDX_DIM ** -0.5
Q_BLOCK = 128
WIDTH_RNN = 512
N_RNN_BLOCKS = 8
RNN_BLOCK = WIDTH_RNN // N_RNN_BLOCKS
RNN_CONV = 4
LRU_C = 8.0
N_MEM = 256
N_HEADS_M = 4
HEAD_DIM_M = 128
WIDTH_M = N_HEADS_M * HEAD_DIM_M
N_BRANCH = 3
BRANCH_WIDTH = 512
D_FF = 3 * D_MODEL
FFN_CONV = 3
ROPE_THETA = 500000.0
ROPE_DIM_A = HEAD_DIM_A // 4
ROPE_DIM_I = IDX_DIM // 4
EPS = 1e-6

IN_SIZES = (WIDTH_A, WIDTH_A, WIDTH_A, N_IDX_HEADS * IDX_DIM, IDX_DIM, N_IDX_HEADS,
            WIDTH_RNN, WIDTH_RNN, WIDTH_M, N_BRANCH * D_MODEL)
IN_SPLITS = tuple(sum(IN_SIZES[:i + 1]) for i in range(len(IN_SIZES) - 1))
D_IN = sum(IN_SIZES)

kernel_name = 'hybrid_dsa_rglru_memxattn_convffn_step'


def rmsnorm(x, g):
    xf = x.astype(jnp.float32)
    y = xf * lax.rsqrt(jnp.mean(xf * xf, axis=-1, keepdims=True) + EPS)
    return (y * g.astype(jnp.float32)).astype(x.dtype)


def rope(x, pos, rot_dim):
    half = rot_dim // 2
    freqs = jnp.power(ROPE_THETA, -jnp.arange(half, dtype=jnp.float32) * 2.0 / rot_dim)
    ang = pos.astype(jnp.float32)[:, None] * freqs[None, :]
    cos = jnp.cos(ang)[:, None, :]
    sin = jnp.sin(ang)[:, None, :]
    xf = x.astype(jnp.float32)
    x1 = xf[..., :half]
    x2 = xf[..., half:rot_dim]
    out = jnp.concatenate([x1 * cos - x2 * sin, x2 * cos + x1 * sin, xf[..., rot_dim:]], axis=-1)
    return out.astype(x.dtype)


def causal_dwconv(x, buf, w, b):
    width = w.shape[0]
    t = x.shape[1]
    xc = jnp.concatenate([buf.astype(x.dtype), x], axis=1)
    out = b + xc[:, 0:t] * w[0]
    for j in range(1, width):
        out = out + xc[:, j:j + t] * w[j]
    return out, xc[:, t:]


def rg_lru(x, h0, w_a, b_a, w_x, b_x, lam):
    bsz, t, c = x.shape
    xb = x.reshape(bsz, t, N_RNN_BLOCKS, RNN_BLOCK)
    r = jax.nn.sigmoid((jnp.einsum('btnc,ncd->btnd', xb, w_a).reshape(bsz, t, c) + b_a).astype(jnp.float32))
    i = jax.nn.sigmoid((jnp.einsum('btnc,ncd->btnd', xb, w_x).reshape(bsz, t, c) + b_x).astype(jnp.float32))
    log_a = -LRU_C * r * jax.nn.softplus(-lam.astype(jnp.float32))
    a = jnp.exp(log_a)
    u = jnp.sqrt(-jnp.expm1(2.0 * log_a)) * i * x.astype(jnp.float32)
    u = u.at[:, 0].add(a[:, 0] * h0.astype(jnp.float32))

    def combine(left, right):
        a_l, b_l = left
        a_r, b_r = right
        return a_l * a_r, a_r * b_l + b_r

    _, h = lax.associative_scan(combine, (a, u), axis=1)
    return h.astype(x.dtype), h[:, -1].astype(x.dtype)


def index_select(qi, wi, ki, q_pos, key_pos, k_sel):
    s = jnp.einsum('bthd,bsd->bths', qi, ki, preferred_element_type=jnp.float32)
    score = jnp.einsum('bths,bth->bts', jax.nn.relu(s), wi.astype(jnp.float32))
    causal = key_pos[None, :] <= q_pos[:, None]
    score = jnp.where(causal[None], score, -jnp.inf)
    vals, idx = lax.top_k(score, k_sel)
    return idx, jnp.isfinite(vals)


def attend_selected(q, ks, vs, valid):
    bsz, t = q.shape[:2]
    s = jnp.einsum('bthd,btkhd->bthk', q, ks, preferred_element_type=jnp.float32) * HEAD_DIM_A ** -0.5
    s = jnp.where(valid[:, :, None, :], s, -jnp.inf)
    p = jax.nn.softmax(s, axis=-1)
    o = jnp.einsum('bthk,btkhd->bthd', p.astype(vs.dtype), vs)
    return o.reshape(bsz, t, WIDTH_A)


def prompt_sparse_attn(q, k, v, qi, ki, wi):
    bsz, seq = q.shape[:2]
    k_sel = min(TOPK_MAX, seq // 4)
    nb = seq // Q_BLOCK
    key_pos = jnp.arange(seq)
    bidx = jnp.arange(bsz)[:, None, None]

    def blockify(a):
        return jnp.moveaxis(a.reshape((bsz, nb, Q_BLOCK) + a.shape[2:]), 1, 0)

    def one_block(args):
        qb, qib, wib, pb = args
        idx, valid = index_select(qib, wib, ki, pb, key_pos, k_sel)
        return attend_selected(qb, k[bidx, idx], v[bidx, idx], valid)

    out = lax.map(one_block, (blockify(q), blockify(qi), blockify(wi), key_pos.reshape(nb, Q_BLOCK)))
    return jnp.moveaxis(out, 0, 1).reshape(bsz, seq, WIDTH_A)


def sample_sparse_attn(q, k, v, qi, ki, wi, cache_k, cache_v, cache_kidx, page_table, layer):
    db, t = q.shape[:2]
    past = page_table.shape[1] * PAGE_SIZE
    n_keys = past + t
    k_sel = min(TOPK_MAX, n_keys // 4)
    ki_past = cache_kidx[layer, page_table].reshape(db, past, IDX_DIM)
    ki_all = jnp.concatenate([ki_past, ki.astype(ki_past.dtype)], axis=1)
    idx, valid = index_select(qi, wi, ki_all, past + jnp.arange(t), jnp.arange(n_keys), k_sel)
    bidx = jnp.arange(db)[:, None, None]
    pidx = jnp.minimum(idx, past - 1)
    phys = page_table[bidx, pidx // PAGE_SIZE]
    off = pidx % PAGE_SIZE
    kp = cache_k[layer, phys, off]
    vp = cache_v[layer, phys, off]
    sel_past = valid & (idx < past)
    sel_new = jnp.any((idx[..., None] == past + jnp.arange(t)) & valid[..., None], axis=2)
    scale = HEAD_DIM_A ** -0.5
    s_past = jnp.einsum('bthd,btkhd->bthk', q, kp, preferred_element_type=jnp.float32) * scale
    s_new = jnp.einsum('bthd,bjhd->bthj', q, k, preferred_element_type=jnp.float32) * scale
    s = jnp.concatenate([jnp.where(sel_past[:, :, None, :], s_past, -jnp.inf),
                         jnp.where(sel_new[:, :, None, :], s_new, -jnp.inf)], axis=-1)
    p = jax.nn.softmax(s, axis=-1).astype(v.dtype)
    o = (jnp.einsum('bthk,btkhd->bthd', p[..., :k_sel], vp)
         + jnp.einsum('bthj,bjhd->bthd', p[..., k_sel:], v))
    return o.reshape(db, t, WIDTH_A)


def memory_kv(mem, g, w, k_g):
    bsz, n = mem.shape[:2]
    mk, mv = jnp.split(rmsnorm(mem, g) @ w, 2, axis=-1)
    mk = rmsnorm(mk.reshape(bsz, n, N_HEADS_M, HEAD_DIM_M), k_g)
    return mk, mv.reshape(bsz, n, N_HEADS_M, HEAD_DIM_M)


def memory_attend(qm, mk, mv):
    bsz, t = qm.shape[:2]
    s = jnp.einsum('bthd,bmhd->bhtm', qm, mk, preferred_element_type=jnp.float32) * HEAD_DIM_M ** -0.5
    p = jax.nn.softmax(s, axis=-1)
    o = jnp.einsum('bhtm,bmhd->bthd', p.astype(mv.dtype), mv)
    return o.reshape(bsz, t, WIDTH_M)


def layer_forward(x, p, pos, attn_fn, mem_k, mem_v, h0, rnn_buf, ffn_buf):
    bsz, t, _ = x.shape
    xn = rmsnorm(x, p['norm_mix_g'])
    z = xn @ p['w_in']
    q, k, v, qi, ki, wi, xr, gr, qm, gates = jnp.split(z, IN_SPLITS, axis=-1)
    q = rope(rmsnorm(q.reshape(bsz, t, N_HEADS_A, HEAD_DIM_A), p['q_norm_g']), pos, ROPE_DIM_A)
    k = rope(rmsnorm(k.reshape(bsz, t, N_HEADS_A, HEAD_DIM_A), p['k_norm_g']), pos, ROPE_DIM_A)
    v = v.reshape(bsz, t, N_HEADS_A, HEAD_DIM_A)
    qi = rope(qi.reshape(bsz, t, N_IDX_HEADS, IDX_DIM), pos, ROPE_DIM_I)
    ki = rope(ki[:, :, None, :], pos, ROPE_DIM_I)[:, :, 0, :]
    wi = wi * INDEX_WEIGHT_SCALE
    a_out = attn_fn(q, k, v, qi, ki, wi)
    xr_c, rnn_buf_new = causal_dwconv(xr, rnn_buf, p['rnn_conv_w'], p['rnn_conv_b'])
    h, h_last = rg_lru(xr_c, h0, p['rnn_wa'], p['rnn_ba'], p['rnn_wx'], p['rnn_bx'], p['rnn_lambda'])
    r_out = h * jax.nn.gelu(gr)
    qm = rmsnorm(qm.reshape(bsz, t, N_HEADS_M, HEAD_DIM_M), p['mq_norm_g'])
    m_out = memory_attend(qm, mem_k, mem_v)
    branches = jnp.stack([a_out, r_out, m_out], axis=2)
    proj = jnp.einsum('btnc,ncd->btnd', branches, p['w_branch'])
    g = jax.nn.sigmoid(gates.reshape(bsz, t, N_BRANCH, D_MODEL))
    x = x + jnp.sum(g * proj, axis=2) @ p['w_out']
    gt, val = jnp.split(rmsnorm(x, p['norm_ffn_g']) @ p['w_ffn_up'], 2, axis=-1)
    gt_c, ffn_buf_new = causal_dwconv(gt, ffn_buf, p['ffn_conv_w'], p['ffn_conv_b'])
    x = x + (jax.nn.gelu(gt_c) * val) @ p['w_ffn_down']
    return x, (k, v, ki), h_last, rnn_buf_new, ffn_buf_new


def setup_inputs(seed: int = 0) -> dict:
    key = jax.random.key(seed)
    keys = iter(jax.random.split(key, 48))
    f32 = jnp.float32

    def nrm(shape, scale=1.0):
        return jax.random.normal(next(keys), shape, f32) * scale

    def gain(shape):
        return 1.0 + nrm(shape, 0.02)

    n_pages = PAST_LEN // PAGE_SIZE
    n_used = DEC_BATCH * n_pages
    n_pool = n_used + n_used // 4
    page_table = jax.random.permutation(next(keys), n_pool)[:n_used].reshape(DEC_BATCH, n_pages).astype(jnp.int32)
    u = jax.random.uniform(next(keys), (DEPTH, WIDTH_RNN), f32, 0.9, 0.999)
    s = u ** (1.0 / LRU_C)
    rnn_lambda = jnp.log(s) - jnp.log1p(-s)
    return {
        'x_prompt': nrm((BATCH, SEQ, D_MODEL)),
        'x_sample': nrm((DEC_BATCH, DEC_SEQ, D_MODEL)),
        'mem_prompt': nrm((BATCH, N_MEM, D_MODEL)),
        'cache_k': nrm((DEPTH, n_pool, PAGE_SIZE, N_HEADS_A, HEAD_DIM_A)),
        'cache_v': nrm((DEPTH, n_pool, PAGE_SIZE, N_HEADS_A, HEAD_DIM_A)),
        'cache_kidx': nrm((DEPTH, n_pool, PAGE_SIZE, IDX_DIM)),
        'page_table': page_table,
        'cache_mem_k': nrm((DEPTH, DEC_BATCH, N_MEM, N_HEADS_M, HEAD_DIM_M)),
        'cache_mem_v': nrm((DEPTH, DEC_BATCH, N_MEM, N_HEADS_M, HEAD_DIM_M)),
        'state_rnn_h': nrm((DEPTH, DEC_BATCH, WIDTH_RNN), 0.5),
        'state_rnn_conv': nrm((DEPTH, DEC_BATCH, RNN_CONV - 1, WIDTH_RNN)),
        'state_ffn_conv': nrm((DEPTH, DEC_BATCH, FFN_CONV - 1, D_FF)),
        'norm_mix_g': gain((DEPTH, D_MODEL)),
        'w_in': nrm((DEPTH, D_MODEL, D_IN), D_MODEL ** -0.5),
        'q_norm_g': gain((DEPTH, HEAD_DIM_A)),
        'k_norm_g': gain((DEPTH, HEAD_DIM_A)),
        'mq_norm_g': gain((DEPTH, HEAD_DIM_M)),
        'mk_norm_g': gain((DEPTH, HEAD_DIM_M)),
        'mem_norm_g': gain((DEPTH, D_MODEL)),
        'w_mem_kv': nrm((DEPTH, D_MODEL, 2 * WIDTH_M), D_MODEL ** -0.5),
        'rnn_conv_w': nrm((DEPTH, RNN_CONV, WIDTH_RNN), RNN_CONV ** -0.5),
        'rnn_conv_b': nrm((DEPTH, WIDTH_RNN), 0.01),
        'rnn_wa': nrm((DEPTH, N_RNN_BLOCKS, RNN_BLOCK, RNN_BLOCK), RNN_BLOCK ** -0.5),
        'rnn_ba': nrm((DEPTH, WIDTH_RNN), 0.01),
        'rnn_wx': nrm((DEPTH, N_RNN_BLOCKS, RNN_BLOCK, RNN_BLOCK), RNN_BLOCK ** -0.5),
        'rnn_bx': nrm((DEPTH, WIDTH_RNN), 0.01),
        'rnn_lambda': rnn_lambda,
        'w_branch': nrm((DEPTH, N_BRANCH, BRANCH_WIDTH, D_MODEL), BRANCH_WIDTH ** -0.5),
        'w_out': nrm((DEPTH, D_MODEL, D_MODEL), D_MODEL ** -0.5),
        'norm_ffn_g': gain((DEPTH, D_MODEL)),
        'w_ffn_up': nrm((DEPTH, D_MODEL, 2 * D_FF), D_MODEL ** -0.5),
        'ffn_conv_w': nrm((DEPTH, FFN_CONV, D_FF), FFN_CONV ** -0.5),
        'ffn_conv_b': nrm((DEPTH, D_FF), 0.01),
        'w_ffn_down': nrm((DEPTH, D_FF, D_MODEL), D_FF ** -0.5),
    }


def reference(x_prompt, x_sample, mem_prompt, cache_k, cache_v, cache_kidx, page_table,
              cache_mem_k, cache_mem_v, state_rnn_h, state_rnn_conv, state_ffn_conv,
              norm_mix_g, w_in, q_norm_g, k_norm_g, mq_norm_g, mk_norm_g, mem_norm_g, w_mem_kv,
              rnn_conv_w, rnn_conv_b, rnn_wa, rnn_ba, rnn_wx, rnn_bx, rnn_lambda,
              w_branch, w_out, norm_ffn_g, w_ffn_up, ffn_conv_w, ffn_conv_b, w_ffn_down):
    bsz, seq = x_prompt.shape[:2]
    t_new = x_sample.shape[1]
    past = page_table.shape[1] * PAGE_SIZE
    pos_p = jnp.arange(seq)
    pos_s = past + jnp.arange(t_new)
    dt = x_prompt.dtype
    h0_p = jnp.zeros((bsz, WIDTH_RNN), dt)
    rbuf_p = jnp.zeros((bsz, RNN_CONV - 1, WIDTH_RNN), dt)
    fbuf_p = jnp.zeros((bsz, FFN_CONV - 1, D_FF), dt)
    xp, xs = x_prompt, x_sample
    kp_l, ks_l, vp_l, vs_l, kip_l, kis_l = [], [], [], [], [], []
    mkp_l, mvp_l, hp_l, hs_l, rbp_l, rbs_l, fbp_l, fbs_l = [], [], [], [], [], [], [], []
    for l in range(DEPTH):
        p = {'norm_mix_g': norm_mix_g[l], 'w_in': w_in[l], 'q_norm_g': q_norm_g[l], 'k_norm_g': k_norm_g[l],
             'mq_norm_g': mq_norm_g[l], 'rnn_conv_w': rnn_conv_w[l], 'rnn_conv_b': rnn_conv_b[l],
             'rnn_wa': rnn_wa[l], 'rnn_ba': rnn_ba[l], 'rnn_wx': rnn_wx[l], 'rnn_bx': rnn_bx[l],
             'rnn_lambda': rnn_lambda[l], 'w_branch': w_branch[l], 'w_out': w_out[l],
             'norm_ffn_g': norm_ffn_g[l], 'w_ffn_up': w_ffn_up[l], 'ffn_conv_w': ffn_conv_w[l],
             'ffn_conv_b': ffn_conv_b[l], 'w_ffn_down': w_ffn_down[l]}
        mk_p, mv_p = memory_kv(mem_prompt, mem_norm_g[l], w_mem_kv[l], mk_norm_g[l])
        xp, (k_p, v_p, ki_p), h_p, rb_p, fb_p = layer_forward(
            xp, p, pos_p, prompt_sparse_attn, mk_p, mv_p, h0_p, rbuf_p, fbuf_p)
        attn_s = functools.partial(sample_sparse_attn, cache_k=cache_k, cache_v=cache_v,
                                   cache_kidx=cache_kidx, page_table=page_table, layer=l)
        xs, (k_s, v_s, ki_s), h_s, rb_s, fb_s = layer_forward(
            xs, p, pos_s, attn_s, cache_mem_k[l], cache_mem_v[l],
            state_rnn_h[l], state_rnn_conv[l], state_ffn_conv[l])
        kp_l.append(k_p); ks_l.append(k_s); vp_l.append(v_p); vs_l.append(v_s)
        kip_l.append(ki_p); kis_l.append(ki_s); mkp_l.append(mk_p); mvp_l.append(mv_p)
        hp_l.append(h_p); hs_l.append(h_s); rbp_l.append(rb_p); rbs_l.append(rb_s)
        fbp_l.append(fb_p); fbs_l.append(fb_s)
    return (xp, xs,
            jnp.stack(kp_l), jnp.stack(ks_l), jnp.stack(vp_l), jnp.stack(vs_l),
            jnp.stack(kip_l), jnp.stack(kis_l), jnp.stack(mkp_l), jnp.stack(mvp_l),
            jnp.stack(hp_l), jnp.stack(hs_l), jnp.stack(rbp_l), jnp.stack(rbs_l),
            jnp.stack(fbp_l), jnp.stack(fbs_l))
```

```python
import functools

import jax
import jax.numpy as jnp
from jax import lax
from jax.experimental import pallas as pl
from jax.experimental.pallas import tpu as pltpu

F32 = jnp.float32
BF16 = jnp.bfloat16

D_MODEL = 1024
DEPTH = 4
PAGE_SIZE = 128
N_HEADS_A = 8
HEAD_DIM_A = 64
WIDTH_A = 512
N_IDX_HEADS = 8
IDX_DIM = 64
TOPK_MAX = 256
INDEX_WEIGHT_SCALE = N_IDX_HEADS ** -0.5 * IDX_DIM ** -0.5
WIDTH_RNN = 512
N_RNN_BLOCKS = 8
RNN_CONV = 4
LRU_C = 8.0
N_MEM = 256
N_HEADS_M = 4
HEAD_DIM_M = 128
WIDTH_M = 512
N_BRANCH = 3
D_FF = 3 * D_MODEL
FFN_CONV = 3
ROPE_THETA = 500000.0
ROPE_DIM = 16
EPS = 1e-6

IN_SIZES = (WIDTH_A, WIDTH_A, WIDTH_A, N_IDX_HEADS * IDX_DIM, IDX_DIM, N_IDX_HEADS,
            WIDTH_RNN, WIDTH_RNN, WIDTH_M, N_BRANCH * D_MODEL)
IN_SPLITS = tuple(sum(IN_SIZES[:i + 1]) for i in range(len(IN_SIZES) - 1))

LANES = 128
SUBLANES = 8
OFF_Q, OFF_K, OFF_V, OFF_QI = 0, 512, 1024, 1536
OFF_KI2 = 2048
OFF_WI = OFF_KI2 + LANES
OFF_XR = OFF_WI + LANES
OFF_GR = OFF_XR + WIDTH_RNN
OFF_QM = OFF_GR + WIDTH_RNN
OFF_G = OFF_QM + WIDTH_M
D_IN_P = OFF_G + N_BRANCH * D_MODEL

VMEM_LIMIT = 56 * 1024 * 1024

NEG_INF = float("-inf")
INT_MIN = -2 ** 31
NEG_INF_KEY = INT_MIN + 0x7FFFFF


def _cparams(sem):
    return pltpu.CompilerParams(dimension_semantics=sem, vmem_limit_bytes=VMEM_LIMIT)


def _const_spec(shape):
    nd = len(shape)
    return pl.BlockSpec(shape, lambda *_: (0,) * nd)


def _gelu(x):
    return 0.5 * x * (1.0 + jnp.tanh(0.7978845608028654 * (x + 0.044715 * (x * x * x))))


def _softplus(z):
    return jnp.maximum(z, 0.0) + jnp.log1p(jnp.exp(-jnp.abs(z)))


def _rmsnorm_rows(x, g):
    return x * lax.rsqrt(jnp.mean(x * x, axis=-1, keepdims=True) + EPS) * g


def _dot(a, b):
    return jnp.dot(a, b, preferred_element_type=F32)


def _dot_nt(a, b):
    return lax.dot_general(a, b, (((1,), (1,)), ((), ())), preferred_element_type=F32)


def _select_bias(score, k_sel, bias_ref):
    rows, s_len = score.shape
    valid = score > NEG_INF
    score = jnp.where(score == 0.0, 0.0, score)
    bits = lax.bitcast_convert_type(score, jnp.int32)
    key = jnp.where(bits < 0, bits ^ jnp.int32(0x7FFFFFFF), bits)
    kf = jnp.float32(k_sel)

    def count_ge(c):
        return jnp.sum(jnp.where(key >= c, 1.0, 0.0), axis=1, keepdims=True)

    thr = jnp.where(count_ge(jnp.int32(0)) >= kf, jnp.int32(0), jnp.int32(INT_MIN))

    def body(it, thr):
        cand = thr + jnp.left_shift(jnp.int32(1), 30 - it)
        return jnp.where(count_ge(cand) >= kf, cand, thr)

    thr = lax.fori_loop(0, 31, body, thr)
    n_ge = count_ge(thr)
    bias_ref[...] = jnp.where((key >= thr) & valid, 0.0, NEG_INF)

    tie = jnp.where((n_ge > kf) & (thr > NEG_INF_KEY), 1.0, 0.0)

    @pl.when(jnp.max(tie) > 0.0)
    def _():
        gt = jnp.where((key > thr) & valid, 1.0, 0.0)
        eq = jnp.where((key == thr) & valid, 1.0, 0.0)
        need = kf - jnp.sum(gt, axis=1, keepdims=True)
        r_i = lax.broadcasted_iota(jnp.int32, (LANES, LANES), 0)
        c_i = lax.broadcasted_iota(jnp.int32, (LANES, LANES), 1)
        tri = jnp.where(r_i < c_i, 1.0, 0.0).astype(BF16)
        run = jnp.zeros((rows, 1), F32)
        for c in range(s_len // LANES):
            sl = slice(c * LANES, (c + 1) * LANES)
            eq_c = eq[:, sl]
            before = _dot(eq_c.astype(BF16), tri) + run
            sel = (gt[:, sl] > 0.0) | ((eq_c > 0.0) & (before < need))
            bias_ref[:, sl] = jnp.where(sel, 0.0, NEG_INF)
            run = run + jnp.sum(eq_c, axis=1, keepdims=True)


def _inproj_kernel(x_ref, g_ref, w_ref, bd64_ref, bd128_ref, qg_ref, kg_ref, mqg_ref,
                   c_ref, s1_ref, s2_ref,
                   q_ref, k_ref, kb_ref, v_ref, vb_ref, qi_ref, ki_ref, ki2_ref, wi_ref,
                   xr_ref, gr_ref, qm_ref, gate_ref):
    xn = _rmsnorm_rows(x_ref[...], g_ref[...]).astype(BF16)

    def seg(off, width):
        return _dot(xn, w_ref[:, off:off + width])

    c1, s1, s2 = c_ref[...], s1_ref[...], s2_ref[...]

    def rope(z):
        w = z.shape[1]
        n = w // LANES
        cc, a, b = (jnp.tile(t, (1, n)) if n > 1 else t for t in (c1, s1, s2))
        return z * cc + pltpu.roll(z, w - ROPE_DIM // 2, 1) * a + pltpu.roll(z, ROPE_DIM // 2, 1) * b

    def headnorm(z, bd_ref, dim, g):
        ss = _dot((z * z).astype(BF16), bd_ref[...]) * (1.0 / dim)
        return z * lax.rsqrt(ss + EPS) * g

    q = rope(headnorm(seg(OFF_Q, WIDTH_A), bd64_ref, HEAD_DIM_A, qg_ref[...]))
    q_ref[...] = (q * HEAD_DIM_A ** -0.5).astype(BF16)
    k = rope(headnorm(seg(OFF_K, WIDTH_A), bd64_ref, HEAD_DIM_A, kg_ref[...]))
    k_ref[...] = k
    kb_ref[...] = k.astype(BF16)
    v = seg(OFF_V, WIDTH_A)
    v_ref[...] = v
    vb_ref[...] = v.astype(BF16)
    qi_ref[...] = rope(seg(OFF_QI, WIDTH_A)).astype(BF16)
    ki2 = rope(seg(OFF_KI2, LANES))
    ki2_ref[...] = ki2.astype(BF16)
    ki_ref[...] = ki2[:, :IDX_DIM]
    wi_ref[...] = seg(OFF_WI, LANES) * INDEX_WEIGHT_SCALE
    xr_ref[...] = seg(OFF_XR, WIDTH_RNN)
    gr_ref[...] = seg(OFF_GR, WIDTH_RNN)
    qm_ref[...] = headnorm(seg(OFF_QM, WIDTH_M), bd128_ref, HEAD_DIM_M, mqg_ref[...]).astype(BF16)
    for c in range(N_BRANCH * D_MODEL // 512):
        gate_ref[:, c * 512:(c + 1) * 512] = jax.nn.sigmoid(seg(OFF_G + c * 512, 512)).astype(BF16)


def _inproj(x, p, rope_tabs, tm):
    n = x.shape[0]
    row = lambda w: pl.BlockSpec((tm, w), lambda i: (i, 0))
    outs = [("q", WIDTH_A, BF16), ("k", WIDTH_A, F32), ("kb", WIDTH_A, BF16), ("v", WIDTH_A, F32),
            ("vb", WIDTH_A, BF16), ("qi", WIDTH_A, BF16), ("ki", IDX_DIM, F32), ("ki2", LANES, BF16),
            ("wi", LANES, F32), ("xr", WIDTH_RNN, F32), ("gr", WIDTH_RNN, F32), ("qm", WIDTH_M, BF16),
            ("gate", N_BRANCH * D_MODEL, BF16)]
    res = pl.pallas_call(
        _inproj_kernel,
        grid=(n // tm,),
        in_specs=[row(D_MODEL), _const_spec((1, D_MODEL)),
                  pl.BlockSpec((D_MODEL, D_IN_P), lambda i: (0, 0), pipeline_mode=pl.Buffered(1)),
                  _const_spec((WIDTH_A, WIDTH_A)), _const_spec((WIDTH_M, WIDTH_M)),
                  _const_spec((1, WIDTH_A)), _const_spec((1, WIDTH_A)), _const_spec((1, WIDTH_M)),
                  row(LANES), row(LANES), row(LANES)],
        out_specs=[row(w) for _, w, _ in outs],
        out_shape=[jax.ShapeDtypeStruct((n, w), dt) for _, w, dt in outs],
        compiler_params=_cparams(("parallel",)),
        name="inproj",
    )(x, p["norm_mix_g"], p["w_in"], p["bd64"], p["bd128"], p["q_norm_g"], p["k_norm_g"],
      p["mq_norm_g"], *rope_tabs)
    return {name: r for (name, _, _), r in zip(outs, res)}


def _memkv_kernel(m_ref, g_ref, w_ref, bd128_ref, kg_ref, mk_ref, mkb_ref, mv_ref, mvb_ref):
    xn = _rmsnorm_rows(m_ref[...], g_ref[...]).astype(BF16)
    zk = _dot(xn, w_ref[:, :WIDTH_M])
    ss = _dot((zk * zk).astype(BF16), bd128_ref[...]) * (1.0 / HEAD_DIM_M)
    mk = zk * lax.rsqrt(ss + EPS) * kg_ref[...]
    mk_ref[...] = mk
    mkb_ref[...] = mk.astype(BF16)
    mv = _dot(xn, w_ref[:, WIDTH_M:])
    mv_ref[...] = mv
    mvb_ref[...] = mv.astype(BF16)


def _memkv(mem, p, tm=512):
    n = mem.shape[0]
    row = lambda w: pl.BlockSpec((tm, w), lambda i: (i, 0))
    return pl.pallas_call(
        _memkv_kernel,
        grid=(n // tm,),
        in_specs=[row(D_MODEL), _const_spec((1, D_MODEL)), _const_spec((D_MODEL, 2 * WIDTH_M)),
                  _const_spec((WIDTH_M, WIDTH_M)), _const_spec((1, WIDTH_M))],
        out_specs=[row(WIDTH_M)] * 4,
        out_shape=[jax.ShapeDtypeStruct((n, WIDTH_M), dt) for dt in (F32, BF16, F32, BF16)],
        compiler_params=_cparams(("parallel",)),
        name="memkv",
    )(mem, p["mem_norm_g"], p["w_mem_kv"], p["bd128"], p["mk_norm_g"])


def _half_masks():
    lane = lax.broadcasted_iota(jnp.int32, (1, LANES), 1)
    return lane < HEAD_DIM_A, lane >= HEAD_DIM_A


def _mem_attend(qm_ref, mk, mv, m_ref):
    for h in range(N_HEADS_M):
        sl = slice(h * HEAD_DIM_M, (h + 1) * HEAD_DIM_M)
        s = _dot_nt(qm_ref[:, sl], mk[:, sl]) * HEAD_DIM_M ** -0.5
        e = jnp.exp(s - jnp.max(s, axis=1, keepdims=True))
        o = _dot(e.astype(BF16), mv[:, sl])
        m_ref[:, sl] = (o * (1.0 / jnp.sum(e, axis=1, keepdims=True))).astype(m_ref.dtype)


def _pattn_kernel(q_ref, qi_ref, wi_ref, qm_ref, kb_ref, vb_ref, ki2_ref, mk_ref, mv_ref,
                  a_ref, m_ref, bias_ref, *, tq, s_len, k_sel):
    i = pl.program_id(1)
    halves = _half_masks()
    ki2 = ki2_ref[...]
    zero = jnp.zeros((), BF16)

    score = jnp.zeros((tq, s_len), F32)
    for hp in range(N_IDX_HEADS // 2):
        qp = qi_ref[:, hp * LANES:(hp + 1) * LANES]
        for half in range(2):
            h = 2 * hp + half
            s = _dot_nt(jnp.where(halves[half], qp, zero), ki2)
            score = score + jnp.maximum(s, 0.0) * wi_ref[:, h:h + 1]
    qpos = i * tq + lax.broadcasted_iota(jnp.int32, (tq, 1), 0)
    kpos = lax.broadcasted_iota(jnp.int32, (1, s_len), 1)
    score = jnp.where(kpos <= qpos, score, NEG_INF)
    _select_bias(score, k_sel, bias_ref)

    for hp in range(N_HEADS_A // 2):
        sl = slice(hp * LANES, (hp + 1) * LANES)
        qp, kp, vp = q_ref[:, sl], kb_ref[:, sl], vb_ref[:, sl]
        acc = jnp.zeros((tq, LANES), F32)
        for half in range(2):
            s = _dot_nt(jnp.where(halves[half], qp, zero), kp) + bias_ref[...]
            e = jnp.exp(s - jnp.max(s, axis=1, keepdims=True))
            o = _dot(e.astype(BF16), jnp.where(halves[half], vp, zero))
            acc = acc + o * (1.0 / jnp.sum(e, axis=1, keepdims=True))
        a_ref[:, sl] = acc.astype(BF16)

    _mem_attend(qm_ref, mk_ref[...], mv_ref[...], m_ref)


def _prompt_attention(z, mkb, mvb, bsz, seq, tq=256):
    nq = seq // tq
    k_sel = min(TOPK_MAX, seq // 4)
    qrow = lambda w: pl.BlockSpec((tq, w), lambda b, i: (b * nq + i, 0))
    brow = lambda rows, w: pl.BlockSpec((rows, w), lambda b, i: (b, 0))
    n = bsz * seq
    return pl.pallas_call(
        functools.partial(_pattn_kernel, tq=tq, s_len=seq, k_sel=k_sel),
        grid=(bsz, nq),
        in_specs=[qrow(WIDTH_A), qrow(WIDTH_A), qrow(LANES), qrow(WIDTH_M),
                  brow(seq, WIDTH_A), brow(seq, WIDTH_A), brow(seq, LANES),
                  brow(N_MEM, WIDTH_M), brow(N_MEM, WIDTH_M)],
        out_specs=[qrow(WIDTH_A), qrow(WIDTH_M)],
        out_shape=[jax.ShapeDtypeStruct((n, WIDTH_A), BF16), jax.ShapeDtypeStruct((n, WIDTH_M), BF16)],
        scratch_shapes=[pltpu.VMEM((tq, seq), F32)],
        compiler_params=_cparams(("parallel", "arbitrary")),
        name="prompt_attn",
    )(z["q"], z["qi"], z["wi"], z["qm"], z["kb"], z["vb"], z["ki2"], mkb, mvb)


def _sidx_kernel(pt_ref, qb_ref, wi_ref, kin_ref, *rest, n_pages, t_new):
    page_refs, out_ref = rest[:n_pages], rest[n_pages]
    qb = qb_ref[...]
    wcol = jnp.concatenate([wi_ref[:, h:h + 1] for h in range(N_IDX_HEADS)], axis=0)

    def chunk_scores(keys_bf16):
        r = jnp.maximum(_dot_nt(qb, keys_bf16), 0.0) * wcol
        sc = r[0:t_new]
        for h in range(1, N_IDX_HEADS):
            sc = sc + r[h * t_new:(h + 1) * t_new]
        return sc

    for j in range(n_pages):
        out_ref[:, j * PAGE_SIZE:(j + 1) * PAGE_SIZE] = chunk_scores(page_refs[j][...].astype(BF16))
    knew = jnp.concatenate([kin_ref[...], jnp.zeros((PAGE_SIZE - t_new, IDX_DIM), F32)], axis=0)
    sc = chunk_scores(knew.astype(BF16))
    tpos = lax.broadcasted_iota(jnp.int32, (t_new, PAGE_SIZE), 0)
    cpos = lax.broadcasted_iota(jnp.int32, (t_new, PAGE_SIZE), 1)
    out_ref[:, n_pages * PAGE_SIZE:] = jnp.where(cpos <= tpos, sc, NEG_INF)


def _select_kernel(score_ref, bias_ref, *, k_sel):
    _select_bias(score_ref[...], k_sel, bias_ref)


def _sattn_kernel(pt_ref, q_ref, bias_ref, kn_ref, vn_ref, qm_ref, mk_ref, mv_ref, *rest,
                  n_pages, t_new):
    k_refs, v_refs = rest[:n_pages], rest[n_pages:2 * n_pages]
    a_ref, m_ref = rest[2 * n_pages:]
    zero = jnp.zeros((), BF16)

    def head_rows(x, n_heads, head_dim):
        rows = n_heads * t_new
        r_h = lax.broadcasted_iota(jnp.int32, (rows, n_heads * head_dim), 0) // t_new
        l_h = lax.broadcasted_iota(jnp.int32, (rows, n_heads * head_dim), 1) // head_dim
        return jnp.where(r_h == l_h, jnp.tile(x, (n_heads, 1)), 0.0).astype(BF16), r_h == l_h

    def head_diag(o, keep, n_heads):
        o = jnp.where(keep, o, 0.0)
        out = o[0:t_new]
        for h in range(1, n_heads):
            out = out + o[h * t_new:(h + 1) * t_new]
        return out

    qblk, keep = head_rows(q_ref[...], N_HEADS_A, HEAD_DIM_A)
    bias = jnp.tile(bias_ref[...], (N_HEADS_A, 1))
    pad = jnp.zeros((PAGE_SIZE - t_new, WIDTH_A), F32)
    knew = jnp.concatenate([kn_ref[...], pad], axis=0).astype(BF16)
    vnew = jnp.concatenate([vn_ref[...], pad], axis=0).astype(BF16)
    chunks = [_dot_nt(qblk, k_refs[j][...].astype(BF16)) for j in range(n_pages)]
    chunks.append(_dot_nt(qblk, knew))
    s = jnp.concatenate(chunks, axis=1) + bias
    e = jnp.exp(s - jnp.max(s, axis=1, keepdims=True))
    linv = 1.0 / jnp.sum(e, axis=1, keepdims=True)
    eb = e.astype(BF16)
    o = _dot(eb[:, n_pages * PAGE_SIZE:], vnew)
    for j in range(n_pages):
        o = o + _dot(eb[:, j * PAGE_SIZE:(j + 1) * PAGE_SIZE], v_refs[j][...].astype(BF16))
    a_ref[...] = head_diag(o * linv, keep, N_HEADS_A)

    qmblk, keep_m = head_rows(qm_ref[...], N_HEADS_M, HEAD_DIM_M)
    sm = _dot_nt(qmblk, mk_ref[...].astype(BF16)) * HEAD_DIM_M ** -0.5
    em = jnp.exp(sm - jnp.max(sm, axis=1, keepdims=True))
    om = _dot(em.astype(BF16), mv_ref[...].astype(BF16)) * (1.0 / jnp.sum(em, axis=1, keepdims=True))
    m_ref[...] = head_diag(om, keep_m, N_HEADS_M)


def _sample_attention(z, layer, page_table, cache_k2, cache_v2, cache_kidx, cache_mem_k2, cache_mem_v2,
                      db, t_new):
    n_pages = page_table.shape[1]
    past = n_pages * PAGE_SIZE
    s_pad = past + PAGE_SIZE
    k_sel = min(TOPK_MAX, (past + t_new) // 4)
    n = db * t_new
    qb = z["qi"].reshape(db, t_new, N_IDX_HEADS, IDX_DIM).transpose(0, 2, 1, 3).reshape(
        db, N_IDX_HEADS * t_new, IDX_DIM)
    brow = lambda w: pl.BlockSpec((t_new, w), lambda b, pt: (b, 0))

    def page_spec(w, j):
        return pl.BlockSpec((None, None, PAGE_SIZE, w), lambda b, pt: (layer, pt[b, j], 0, 0))

    score = pl.pallas_call(
        functools.partial(_sidx_kernel, n_pages=n_pages, t_new=t_new),
        grid_spec=pltpu.PrefetchScalarGridSpec(
            num_scalar_prefetch=1, grid=(db,),
            in_specs=[pl.BlockSpec((None, N_IDX_HEADS * t_new, IDX_DIM), lambda b, pt: (b, 0, 0)),
                      brow(LANES), brow(IDX_DIM)] + [page_spec(IDX_DIM, j) for j in range(n_pages)],
            out_specs=brow(s_pad)),
        out_shape=jax.ShapeDtypeStruct((n, s_pad), F32),
        compiler_params=_cparams(("parallel",)),
        name="sample_index",
    )(page_table, qb, z["wi"], z["ki"], *([cache_kidx] * n_pages))

    tr = 256
    bias = pl.pallas_call(
        functools.partial(_select_kernel, k_sel=k_sel),
        grid=(n // tr,),
        in_specs=[pl.BlockSpec((tr, s_pad), lambda i: (i, 0))],
        out_specs=pl.BlockSpec((tr, s_pad), lambda i: (i, 0)),
        out_shape=jax.ShapeDtypeStruct((n, s_pad), F32),
        compiler_params=_cparams(("parallel",)),
        name="sample_select",
    )(score)

    mem_spec = pl.BlockSpec((None, None, N_MEM, WIDTH_M), lambda b, pt: (layer, b, 0, 0))
    return pl.pallas_call(
        functools.partial(_sattn_kernel, n_pages=n_pages, t_new=t_new),
        grid_spec=pltpu.PrefetchScalarGridSpec(
            num_scalar_prefetch=1, grid=(db,),
            in_specs=[brow(WIDTH_A), brow(s_pad), brow(WIDTH_A), brow(WIDTH_A), brow(WIDTH_M),
                      mem_spec, mem_spec]
                     + [page_spec(WIDTH_A, j) for j in range(n_pages)] * 2,
            out_specs=[brow(WIDTH_A), brow(WIDTH_M)]),
        out_shape=[jax.ShapeDtypeStruct((n, WIDTH_A), F32), jax.ShapeDtypeStruct((n, WIDTH_M), F32)],
        compiler_params=_cparams(("parallel",)),
        name="sample_attn",
    )(page_table, z["q"].astype(F32), bias, z["k"], z["v"], z["qm"].astype(F32),
      cache_mem_k2, cache_mem_v2, *([cache_k2] * n_pages), *([cache_v2] * n_pages))


def _lru_gates(xc, wa_ref, ba_ref, wx_ref, bx_ref, lam_ref):
    xb = xc.astype(BF16)
    r = jax.nn.sigmoid(_dot(xb, wa_ref[...]) + ba_ref[...])
    ig = jax.nn.sigmoid(_dot(xb, wx_ref[...]) + bx_ref[...])
    log_a = -LRU_C * r * _softplus(-lam_ref[...])
    a = jnp.exp(log_a)
    u = jnp.sqrt(-jnp.tanh(log_a) * (a * a + 1.0)) * ig * xc
    return a, u


def _rnn_prompt_kernel(xr_ref, gr_ref, cw_ref, cb_ref, wa_ref, ba_ref, wx_ref, bx_ref, lam_ref,
                       r_ref, h_ref, buf_ref, prev_s, hc_s, *, tt):
    @pl.when(pl.program_id(1) == 0)
    def _():
        prev_s[...] = jnp.zeros_like(prev_s)
        hc_s[...] = jnp.zeros_like(hc_s)

    x = xr_ref[...]
    ext = jnp.concatenate([prev_s[...], x], axis=0)
    conv = cb_ref[...] + cw_ref[RNN_CONV - 1:RNN_CONV] * ext
    for j in range(1, RNN_CONV):
        conv = conv + cw_ref[RNN_CONV - 1 - j:RNN_CONV - j] * pltpu.roll(ext, j, 0)
    xc = conv[SUBLANES:]
    a, u = _lru_gates(xc, wa_ref, ba_ref, wx_ref, bx_ref, lam_ref)

    row = lax.broadcasted_iota(jnp.int32, (tt, 1), 0)
    step = 1
    while step < tt:
        m = row >= step
        u = jnp.where(m, a * pltpu.roll(u, step, 0) + u, u)
        a = jnp.where(m, a * pltpu.roll(a, step, 0), a)
        step *= 2
    h = u + a * hc_s[0:1]
    r_ref[...] = (h * _gelu(gr_ref[...])).astype(BF16)
    h_tail = h[tt - SUBLANES:]
    h_ref[...] = h_tail
    hc_s[...] = jnp.broadcast_to(h_tail[SUBLANES - 1:SUBLANES], hc_s.shape)
    buf_ref[...] = x[tt - SUBLANES:]
    prev_s[...] = x[tt - SUBLANES:]


def _rnn_prompt(xr, gr, p, bsz, seq, tt=256):
    nt = seq // tt
    row = pl.BlockSpec((tt, WIDTH_RNN), lambda b, i: (b * nt + i, 0))
    tail = pl.BlockSpec((None, SUBLANES, WIDTH_RNN), lambda b, i: (b, 0, 0))
    vec = _const_spec((1, WIDTH_RNN))
    sq = _const_spec((WIDTH_RNN, WIDTH_RNN))
    return pl.pallas_call(
        functools.partial(_rnn_prompt_kernel, tt=tt),
        grid=(bsz, nt),
        in_specs=[row, row, _const_spec((RNN_CONV, WIDTH_RNN)), vec, sq, vec, sq, vec, vec],
        out_specs=[row, tail, tail],
        out_shape=[jax.ShapeDtypeStruct((bsz * seq, WIDTH_RNN), BF16),
                   jax.ShapeDtypeStruct((bsz, SUBLANES, WIDTH_RNN), F32),
                   jax.ShapeDtypeStruct((bsz, SUBLANES, WIDTH_RNN), F32)],
        scratch_shapes=[pltpu.VMEM((SUBLANES, WIDTH_RNN), F32), pltpu.VMEM((SUBLANES, WIDTH_RNN), F32)],
        compiler_params=_cparams(("parallel", "arbitrary")),
        name="rnn_prompt",
    )(xr, gr, p["rnn_conv_w"], p["rnn_conv_b"], p["rnn_wa"], p["rnn_ba"], p["rnn_wx"], p["rnn_bx"],
      p["rnn_lambda"])


def _rnn_sample_kernel(xr_ref, gr_ref, cbuf_ref, h0_ref, cw_ref, cb_ref, wa_ref, ba_ref, wx_ref, bx_ref,
                       lam_ref, r_ref, h_ref, buf_ref, *, db, t_new):
    x = xr_ref[...]
    n = db * t_new
    ext = jnp.concatenate([cbuf_ref[...], x], axis=0)
    conv = cb_ref[...] + cw_ref[0:1] * ext[0:n]
    for j in range(1, RNN_CONV):
        conv = conv + cw_ref[j:j + 1] * ext[j * db:j * db + n]
    a, u = _lru_gates(conv, wa_ref, ba_ref, wx_ref, bx_ref, lam_ref)
    h = h0_ref[...]
    hs = []
    for t in range(t_new):
        h = a[t * db:(t + 1) * db] * h + u[t * db:(t + 1) * db]
        hs.append(h)
    r_ref[...] = jnp.concatenate(hs, axis=0) * _gelu(gr_ref[...])
    h_ref[...] = h
    buf_ref[...] = ext[n:]


def _rnn_sample(xr_tm, gr_tm, cbuf_tm, h0, p, db, t_new):
    n = db * t_new
    nb = (RNN_CONV - 1) * db
    full = lambda r, c: _const_spec((r, c))
    vec = full(1, WIDTH_RNN)
    sq = full(WIDTH_RNN, WIDTH_RNN)
    return pl.pallas_call(
        functools.partial(_rnn_sample_kernel, db=db, t_new=t_new),
        grid=(1,),
        in_specs=[full(n, WIDTH_RNN), full(n, WIDTH_RNN), full(nb, WIDTH_RNN), full(db, WIDTH_RNN),
                  full(RNN_CONV, WIDTH_RNN), vec, sq, vec, sq, vec, vec],
        out_specs=[full(n, WIDTH_RNN), full(db, WIDTH_RNN), full(nb, WIDTH_RNN)],
        out_shape=[jax.ShapeDtypeStruct((n, WIDTH_RNN), F32), jax.ShapeDtypeStruct((db, WIDTH_RNN), F32),
                   jax.ShapeDtypeStruct((nb, WIDTH_RNN), F32)],
        compiler_params=_cparams(("arbitrary",)),
        name="rnn_sample",
    )(xr_tm, gr_tm, cbuf_tm, h0, p["rnn_conv_w"], p["rnn_conv_b"], p["rnn_wa"], p["rnn_ba"],
      p["rnn_wx"], p["rnn_bx"], p["rnn_lambda"])


def _merge_kernel(x_ref, a_ref, r_ref, m_ref, gate_ref, wb_ref, wo_ref, out_ref):
    y = None
    for n, br in enumerate((a_ref, r_ref, m_ref)):
        proj = _dot(br[...].astype(BF16), wb_ref[n])
        t = gate_ref[:, n * D_MODEL:(n + 1) * D_MODEL].astype(F32) * proj
        y = t if y is None else y + t
    out_ref[...] = x_ref[...] + _dot(y.astype(BF16), wo_ref[...])


def _merge(x, a, r, m, gate, p, tm):
    n = x.shape[0]
    row = lambda w: pl.BlockSpec((tm, w), lambda i: (i, 0))
    return pl.pallas_call(
        _merge_kernel,
        grid=(n // tm,),
        in_specs=[row(D_MODEL), row(WIDTH_A), row(WIDTH_RNN), row(WIDTH_M), row(N_BRANCH * D_MODEL),
                  _const_spec((N_BRANCH, WIDTH_A, D_MODEL)), _const_spec((D_MODEL, D_MODEL))],
        out_specs=row(D_MODEL),
        out_shape=jax.ShapeDtypeStruct((n, D_MODEL), F32),
        compiler_params=_cparams(("parallel",)),
        name="merge",
    )(x, a, r, m, gate, p["w_branch"], p["w_out"])


def _ffn_kernel(*refs, tm, stride, n_prev, use_state):
    if use_state:
        x_ref, g_ref, wg_ref, wv_ref, wd_ref, cw_ref, cb_ref, st_ref, out_ref, fb_ref, xn_s = refs
    else:
        x_ref, g_ref, wg_ref, wv_ref, wd_ref, cw_ref, cb_ref, out_ref, fb_ref, xn_s, carry_s = refs
    i = pl.program_id(1)
    j = pl.program_id(2)

    @pl.when(j == 0)
    def _():
        x = x_ref[...]
        xn_s[...] = _rmsnorm_rows(x, g_ref[...]).astype(BF16)
        out_ref[...] = x

    xn = xn_s[...]
    gt = _dot(xn, wg_ref[...])
    val = _dot(xn, wv_ref[...])
    if use_state:
        prev = st_ref[...]
    else:
        @pl.when(i == 0)
        def _():
            carry_s[j] = jnp.zeros((SUBLANES, gt.shape[1]), F32)

        prev = carry_s[j]
    ext = jnp.concatenate([prev, gt], axis=0)
    conv = cb_ref[...] + cw_ref[FFN_CONV - 1:FFN_CONV] * ext
    for s in range(1, FFN_CONV):
        conv = conv + cw_ref[FFN_CONV - 1 - s:FFN_CONV - s] * pltpu.roll(ext, s * stride, 0)
    act = _gelu(conv[n_prev:]) * val
    out_ref[...] += _dot(act.astype(BF16), wd_ref[...])
    fb_ref[...] = gt[tm - n_prev:]
    if not use_state:
        carry_s[j] = gt[tm - n_prev:]


def _ffn(x, p, *, groups, rows_per_group, tm, stride, state=None, tf=512):
    nt = rows_per_group // tm
    nj = D_FF // tf
    use_state = state is not None
    n_prev = (FFN_CONV - 1) * stride if use_state else SUBLANES
    n = groups * rows_per_group
    xrow = pl.BlockSpec((tm, D_MODEL), lambda b, i, j: (b * nt + i, 0))
    in_specs = [xrow, _const_spec((1, D_MODEL)),
                pl.BlockSpec((D_MODEL, tf), lambda b, i, j: (0, j)),
                pl.BlockSpec((D_MODEL, tf), lambda b, i, j: (0, nj + j)),
                pl.BlockSpec((tf, D_MODEL), lambda b, i, j: (j, 0)),
                pl.BlockSpec((FFN_CONV, tf), lambda b, i, j: (0, j)),
                pl.BlockSpec((1, tf), lambda b, i, j: (0, j))]
    args = [x, p["norm_ffn_g"], p["w_ffn_up"], p["w_ffn_up"], p["w_ffn_down"], p["ffn_conv_w"],
            p["ffn_conv_b"]]
    scratch = [pltpu.VMEM((tm, D_MODEL), BF16)]
    if use_state:
        in_specs.append(pl.BlockSpec((n_prev, tf), lambda b, i, j: (0, j)))
        args.append(state)
    else:
        scratch.append(pltpu.VMEM((nj, SUBLANES, tf), F32))
    return pl.pallas_call(
        functools.partial(_ffn_kernel, tm=tm, stride=stride, n_prev=n_prev, use_state=use_state),
        grid=(groups, nt, nj),
        in_specs=in_specs,
        out_specs=[xrow, pl.BlockSpec((None, n_prev, tf), lambda b, i, j: (b * nt + i, 0, j))],
        out_shape=[jax.ShapeDtypeStruct((n, D_MODEL), F32),
                   jax.ShapeDtypeStruct((groups * nt, n_prev, D_FF), F32)],
        scratch_shapes=scratch,
        compiler_params=_cparams(("parallel", "arbitrary", "arbitrary")),
        name="ffn",
    )(*args)


def _rope_tables(pos):
    half = ROPE_DIM // 2
    freqs = jnp.power(ROPE_THETA, -jnp.arange(half, dtype=F32) * 2.0 / ROPE_DIM)
    ang = pos.astype(F32)[:, None] * freqs[None, :]
    d = jnp.arange(LANES) % HEAD_DIM_A
    cos = jnp.cos(ang)[:, d % half]
    sin = jnp.sin(ang)[:, d % half]
    c = jnp.where(d < ROPE_DIM, cos, 1.0)
    s1 = jnp.where(d < half, -sin, 0.0)
    s2 = jnp.where((d >= half) & (d < ROPE_DIM), sin, 0.0)
    return c, s1, s2


def _block_diag(w):
    nl, nb, c, _ = w.shape
    eye = jnp.eye(nb, dtype=w.dtype)
    return (w[:, :, :, None, :] * eye[None, :, None, :, None]).reshape(nl, nb * c, nb * c)


def _prep_params(norm_mix_g, w_in, q_norm_g, k_norm_g, mq_norm_g, mk_norm_g, mem_norm_g, w_mem_kv,
                 rnn_conv_w, rnn_conv_b, rnn_wa, rnn_ba, rnn_wx, rnn_bx, rnn_lambda,
                 w_branch, w_out, norm_ffn_g, w_ffn_up, ffn_conv_w, ffn_conv_b, w_ffn_down):
    q, k, v, qi, ki, wi, xr, gr, qm, gates = jnp.split(w_in, IN_SPLITS, axis=-1)
    wi_p = jnp.pad(wi, ((0, 0), (0, 0), (0, LANES - N_IDX_HEADS)))
    w_in_p = jnp.concatenate([q, k, v, qi, ki, ki, wi_p, xr, gr, qm, gates], axis=-1).astype(BF16)
    row = lambda a: a[:, None, :]
    stacked = {
        "norm_mix_g": row(norm_mix_g), "w_in": w_in_p,
        "q_norm_g": row(jnp.tile(q_norm_g, (1, N_HEADS_A))), "k_norm_g": row(jnp.tile(k_norm_g, (1, N_HEADS_A))),
        "mq_norm_g": row(jnp.tile(mq_norm_g, (1, N_HEADS_M))), "mk_norm_g": row(jnp.tile(mk_norm_g, (1, N_HEADS_M))),
        "mem_norm_g": row(mem_norm_g), "w_mem_kv": w_mem_kv.astype(BF16),
        "rnn_conv_w": rnn_conv_w, "rnn_conv_b": row(rnn_conv_b),
        "rnn_wa": _block_diag(rnn_wa).astype(BF16), "rnn_ba": row(rnn_ba),
        "rnn_wx": _block_diag(rnn_wx).astype(BF16), "rnn_bx": row(rnn_bx),
        "rnn_lambda": row(rnn_lambda),
        "w_branch": w_branch.astype(BF16), "w_out": w_out.astype(BF16),
        "norm_ffn_g": row(norm_ffn_g), "w_ffn_up": w_ffn_up.astype(BF16),
        "ffn_conv_w": ffn_conv_w, "ffn_conv_b": row(ffn_conv_b), "w_ffn_down": w_ffn_down.astype(BF16),
    }
    ones = lambda n, c: jnp.kron(jnp.eye(n, dtype=F32), jnp.ones((c, c), F32)).astype(BF16)
    shared = {"bd64": ones(N_HEADS_A, HEAD_DIM_A), "bd128": ones(N_HEADS_M, HEAD_DIM_M)}
    return [dict({k_: v_[l] for k_, v_ in stacked.items()}, **shared) for l in range(DEPTH)]


def kernel(x_prompt, x_sample, mem_prompt, cache_k, cache_v, cache_kidx, page_table, cache_mem_k, cache_mem_v, state_rnn_h, state_rnn_conv, state_ffn_conv, norm_mix_g, w_in, q_norm_g, k_norm_g, mq_norm_g, mk_norm_g, mem_norm_g, w_mem_kv, rnn_conv_w, rnn_conv_b, rnn_wa, rnn_ba, rnn_wx, rnn_bx, rnn_lambda, w_branch, w_out, norm_ffn_g, w_ffn_up, ffn_conv_w, ffn_conv_b, w_ffn_down):
    bsz, seq, _ = x_prompt.shape
    db, t_new, _ = x_sample.shape
    n_pool = cache_k.shape[1]
    past = page_table.shape[1] * PAGE_SIZE
    params = _prep_params(norm_mix_g, w_in, q_norm_g, k_norm_g, mq_norm_g, mk_norm_g, mem_norm_g, w_mem_kv,
                          rnn_conv_w, rnn_conv_b, rnn_wa, rnn_ba, rnn_wx, rnn_bx, rnn_lambda,
                          w_branch, w_out, norm_ffn_g, w_ffn_up, ffn_conv_w, ffn_conv_b, w_ffn_down)
    tabs_p = _rope_tables(jnp.tile(jnp.arange(seq), bsz))
    tabs_s = _rope_tables(jnp.tile(past + jnp.arange(t_new), db))
    cache_k2 = cache_k.reshape(DEPTH, n_pool, PAGE_SIZE, WIDTH_A)
    cache_v2 = cache_v.reshape(DEPTH, n_pool, PAGE_SIZE, WIDTH_A)
    cache_mem_k2 = cache_mem_k.reshape(DEPTH, db, N_MEM, WIDTH_M)
    cache_mem_v2 = cache_mem_v.reshape(DEPTH, db, N_MEM, WIDTH_M)
    mem = mem_prompt.reshape(bsz * N_MEM, D_MODEL)

    def to_tm(a):
        return a.reshape(db, t_new, -1).transpose(1, 0, 2).reshape(db * t_new, -1)

    def to_bm(a):
        return a.reshape(t_new, db, -1).transpose(1, 0, 2).reshape(db * t_new, -1)

    xp = x_prompt.reshape(bsz * seq, D_MODEL)
    xs = x_sample.reshape(db * t_new, D_MODEL)
    outs = [[] for _ in range(14)]
    for l in range(DEPTH):
        p = params[l]
        mk, mkb, mv, mvb = _memkv(mem, p)
        zp = _inproj(xp, p, tabs_p, tm=512)
        a_p, m_p = _prompt_attention(zp, mkb, mvb, bsz, seq)
        r_p, h_p, rb_p = _rnn_prompt(zp["xr"], zp["gr"], p, bsz, seq)
        xp = _merge(xp, a_p, r_p, m_p, zp["gate"], p, tm=512)
        xp, fb_p = _ffn(xp, p, groups=bsz, rows_per_group=seq, tm=512, stride=1)
        zs = _inproj(xs, p, tabs_s, tm=512)
        a_s, m_s = _sample_attention(zs, l, page_table, cache_k2, cache_v2, cache_kidx,
                                     cache_mem_k2, cache_mem_v2, db, t_new)
        cbuf_tm = state_rnn_conv[l].transpose(1, 0, 2).reshape((RNN_CONV - 1) * db, WIDTH_RNN)
        r_s_tm, h_s, rb_s_tm = _rnn_sample(to_tm(zs["xr"]), to_tm(zs["gr"]), cbuf_tm, state_rnn_h[l],
                                            p, db, t_new)
        xs = _merge(xs, a_s, to_bm(r_s_tm), m_s, zs["gate"], p, tm=512)
        fst_tm = state_ffn_conv[l].transpose(1, 0, 2).reshape((FFN_CONV - 1) * db, D_FF)
        xs_tm, fb_s_tm = _ffn(to_tm(xs), p, groups=1, rows_per_group=db * t_new, tm=db * t_new,
                              stride=db, state=fst_tm)
        xs = to_bm(xs_tm)

        new = (zp["k"].reshape(bsz, seq, N_HEADS_A, HEAD_DIM_A), zs["k"].reshape(db, t_new, N_HEADS_A, HEAD_DIM_A),
               zp["v"].reshape(bsz, seq, N_HEADS_A, HEAD_DIM_A), zs["v"].reshape(db, t_new, N_HEADS_A, HEAD_DIM_A),
               zp["ki"].reshape(bsz, seq, IDX_DIM), zs["ki"].reshape(db, t_new, IDX_DIM),
               mk.reshape(bsz, N_MEM, N_HEADS_M, HEAD_DIM_M), mv.reshape(bsz, N_MEM, N_HEADS_M, HEAD_DIM_M),
               h_p[:, SUBLANES - 1], h_s,
               rb_p[:, SUBLANES - (RNN_CONV - 1):],
               rb_s_tm.reshape(RNN_CONV - 1, db, WIDTH_RNN).transpose(1, 0, 2),
               fb_p.reshape(bsz, -1, SUBLANES, D_FF)[:, -1, SUBLANES - (FFN_CONV - 1):],
               fb_s_tm.reshape(FFN_CONV - 1, db, D_FF).transpose(1, 0, 2))
        for lst, val in zip(outs, new):
            lst.append(val)
    return (xp.reshape(bsz, seq, D_MODEL), xs.reshape(db, t_new, D_MODEL)) + tuple(jnp.stack(o) for o in outs)
```

```python
import functools

import jax
import jax.numpy as jnp
from jax import lax
from jax.experimental import pallas as pl
from jax.experimental.pallas import tpu as pltpu

F32 = jnp.float32
BF16 = jnp.bfloat16

D_MODEL = 1024
DEPTH = 4
PAGE_SIZE = 128
N_HEADS_A = 8
HEAD_DIM_A = 64
WIDTH_A = 512
N_IDX_HEADS = 8
IDX_DIM = 64
TOPK_MAX = 256
INDEX_WEIGHT_SCALE = N_IDX_HEADS ** -0.5 * IDX_DIM ** -0.5
WIDTH_RNN = 512
N_RNN_BLOCKS = 8
RNN_CONV = 4
LRU_C = 8.0
N_MEM = 256
N_HEADS_M = 4
HEAD_DIM_M = 128
WIDTH_M = 512
N_BRANCH = 3
D_FF = 3 * D_MODEL
FFN_CONV = 3
ROPE_THETA = 500000.0
ROPE_DIM = 16
EPS = 1e-6

IN_SIZES = (WIDTH_A, WIDTH_A, WIDTH_A, N_IDX_HEADS * IDX_DIM, IDX_DIM, N_IDX_HEADS,
            WIDTH_RNN, WIDTH_RNN, WIDTH_M, N_BRANCH * D_MODEL)
IN_SPLITS = tuple(sum(IN_SIZES[:i + 1]) for i in range(len(IN_SIZES) - 1))

LANES = 128
SUBLANES = 8
OFF_Q, OFF_K, OFF_V, OFF_QI = 0, 512, 1024, 1536
OFF_KI2 = 2048
OFF_WI = OFF_KI2 + LANES
OFF_XR = OFF_WI + LANES
OFF_GR = OFF_XR + WIDTH_RNN
OFF_QM = OFF_GR + WIDTH_RNN
OFF_G = OFF_QM + WIDTH_M
D_IN_P = OFF_G + N_BRANCH * D_MODEL

VMEM_LIMIT = 56 * 1024 * 1024

NEG_INF = float("-inf")
INT_MIN = -2 ** 31
NEG_INF_KEY = INT_MIN + 0x7FFFFF


def _cparams(sem):
    return pltpu.CompilerParams(dimension_semantics=sem, vmem_limit_bytes=VMEM_LIMIT)


def _const_spec(shape):
    nd = len(shape)
    return pl.BlockSpec(shape, lambda *_: (0,) * nd)


def _gelu(x):
    return 0.5 * x * (1.0 + jnp.tanh(0.7978845608028654 * (x + 0.044715 * (x * x * x))))


def _softplus(z):
    return jnp.maximum(z, 0.0) + jnp.log1p(jnp.exp(-jnp.abs(z)))


def _rmsnorm_rows(x, g):
    return x * lax.rsqrt(jnp.mean(x * x, axis=-1, keepdims=True) + EPS) * g


def _dot(a, b):
    return jnp.dot(a, b, preferred_element_type=F32)


def _dot_nt(a, b):
    return lax.dot_general(a, b, (((1,), (1,)), ((), ())), preferred_element_type=F32)


def _select_bias(score, k_sel, bias_ref):
    rows, s_len = score.shape
    valid = score > NEG_INF
    score = jnp.where(score == 0.0, 0.0, score)
    bits = lax.bitcast_convert_type(score, jnp.int32)
    key = jnp.where(bits < 0, bits ^ jnp.int32(0x7FFFFFFF), bits)
    kf = jnp.float32(k_sel)

    def count_ge(c):
        return jnp.sum(jnp.where(key >= c, 1.0, 0.0), axis=1, keepdims=True)

    thr = jnp.where(count_ge(jnp.int32(0)) >= kf, jnp.int32(0), jnp.int32(INT_MIN))

    def body(it, thr):
        cand = thr + jnp.left_shift(jnp.int32(1), 30 - it)
        return jnp.where(count_ge(cand) >= kf, cand, thr)

    thr = lax.fori_loop(0, 31, body, thr)
    n_ge = count_ge(thr)
    bias_ref[...] = jnp.where((key >= thr) & valid, 0.0, NEG_INF)

    tie = jnp.where((n_ge > kf) & (thr > NEG_INF_KEY), 1.0, 0.0)

    @pl.when(jnp.max(tie) > 0.0)
    def _():
        gt = jnp.where((key > thr) & valid, 1.0, 0.0)
        eq = jnp.where((key == thr) & valid, 1.0, 0.0)
        need = kf - jnp.sum(gt, axis=1, keepdims=True)
        r_i = lax.broadcasted_iota(jnp.int32, (LANES, LANES), 0)
        c_i = lax.broadcasted_iota(jnp.int32, (LANES, LANES), 1)
        tri = jnp.where(r_i < c_i, 1.0, 0.0).astype(BF16)
        run = jnp.zeros((rows, 1), F32)
        for c in range(s_len // LANES):
            sl = slice(c * LANES, (c + 1) * LANES)
            eq_c = eq[:, sl]
            before = _dot(eq_c.astype(BF16), tri) + run
            sel = (gt[:, sl] > 0.0) | ((eq_c > 0.0) & (before < need))
            bias_ref[:, sl] = jnp.where(sel, 0.0, NEG_INF)
            run = run + jnp.sum(eq_c, axis=1, keepdims=True)


def _inproj_kernel(x_ref, g_ref, w_ref, bd64_ref, bd128_ref, qg_ref, kg_ref, mqg_ref,
                   c_ref, s1_ref, s2_ref,
                   q_ref, k_ref, kb_ref, v_ref, vb_ref, qi_ref, ki_ref, ki2_ref, wi_ref,
                   xr_ref, gr_ref, qm_ref, gate_ref):
    xn = _rmsnorm_rows(x_ref[...], g_ref[...]).astype(BF16)

    def seg(off, width):
        return _dot(xn, w_ref[:, off:off + width])

    c1, s1, s2 = c_ref[...], s1_ref[...], s2_ref[...]

    def rope(z):
        w = z.shape[1]
        n = w // LANES
        cc, a, b = (jnp.tile(t, (1, n)) if n > 1 else t for t in (c1, s1, s2))
        return z * cc + pltpu.roll(z, w - ROPE_DIM // 2, 1) * a + pltpu.roll(z, ROPE_DIM // 2, 1) * b

    def headnorm(z, bd_ref, dim, g):
        ss = _dot((z * z).astype(BF16), bd_ref[...]) * (1.0 / dim)
        return z * lax.rsqrt(ss + EPS) * g

    q = rope(headnorm(seg(OFF_Q, WIDTH_A), bd64_ref, HEAD_DIM_A, qg_ref[...]))
    q_ref[...] = (q * HEAD_DIM_A ** -0.5).astype(BF16)
    k = rope(headnorm(seg(OFF_K, WIDTH_A), bd64_ref, HEAD_DIM_A, kg_ref[...]))
    k_ref[...] = k
    kb_ref[...] = k.astype(BF16)
    v = seg(OFF_V, WIDTH_A)
    v_ref[...] = v
    vb_ref[...] = v.astype(BF16)
    qi_ref[...] = rope(seg(OFF_QI, WIDTH_A)).astype(BF16)
    ki2 = rope(seg(OFF_KI2, LANES))
    ki2_ref[...] = ki2.astype(BF16)
    ki_ref[...] = ki2[:, :IDX_DIM]
    wi_ref[...] = seg(OFF_WI, LANES) * INDEX_WEIGHT_SCALE
    xr_ref[...] = seg(OFF_XR, WIDTH_RNN)
    gr_ref[...] = seg(OFF_GR, WIDTH_RNN)
    qm_ref[...] = headnorm(seg(OFF_QM, WIDTH_M), bd128_ref, HEAD_DIM_M, mqg_ref[...]).astype(BF16)
    for c in range(N_BRANCH * D_MODEL // 512):
        gate_ref[:, c * 512:(c + 1) * 512] = jax.nn.sigmoid(seg(OFF_G + c * 512, 512)).astype(BF16)


def _inproj(x, p, rope_tabs, tm):
    n = x.shape[0]
    row = lambda w: pl.BlockSpec((tm, w), lambda i: (i, 0))
    outs = [("q", WIDTH_A, BF16), ("k", WIDTH_A, F32), ("kb", WIDTH_A, BF16), ("v", WIDTH_A, F32),
            ("vb", WIDTH_A, BF16), ("qi", WIDTH_A, BF16), ("ki", IDX_DIM, F32), ("ki2", LANES, BF16),
            ("wi", LANES, F32), ("xr", WIDTH_RNN, F32), ("gr", WIDTH_RNN, F32), ("qm", WIDTH_M, BF16),
            ("gate", N_BRANCH * D_MODEL, BF16)]
    res = pl.pallas_call(
        _inproj_kernel,
        grid=(n // tm,),
        in_specs=[row(D_MODEL), _const_spec((1, D_MODEL)),
                  pl.BlockSpec((D_MODEL, D_IN_P), lambda i: (0, 0), pipeline_mode=pl.Buffered(1)),
                  _const_spec((WIDTH_A, WIDTH_A)), _const_spec((WIDTH_M, WIDTH_M)),
                  _const_spec((1, WIDTH_A)), _const_spec((1, WIDTH_A)), _const_spec((1, WIDTH_M)),
                  row(LANES), row(LANES), row(LANES)],
        out_specs=[row(w) for _, w, _ in outs],
        out_shape=[jax.ShapeDtypeStruct((n, w), dt) for _, w, dt in outs],
        compiler_params=_cparams(("parallel",)),
        name="inproj",
    )(x, p["norm_mix_g"], p["w_in"], p["bd64"], p["bd128"], p["q_norm_g"], p["k_norm_g"],
      p["mq_norm_g"], *rope_tabs)
    return {name: r for (name, _, _), r in zip(outs, res)}


def _memkv_kernel(m_ref, g_ref, w_ref, bd128_ref, kg_ref, mk_ref, mkb_ref, mv_ref, mvb_ref):
    xn = _rmsnorm_rows(m_ref[...], g_ref[...]).astype(BF16)
    zk = _dot(xn, w_ref[:, :WIDTH_M])
    ss = _dot((zk * zk).astype(BF16), bd128_ref[...]) * (1.0 / HEAD_DIM_M)
    mk = zk * lax.rsqrt(ss + EPS) * kg_ref[...]
    mk_ref[...] = mk
    mkb_ref[...] = mk.astype(BF16)
    mv = _dot(xn, w_ref[:, WIDTH_M:])
    mv_ref[...] = mv
    mvb_ref[...] = mv.astype(BF16)


def _memkv(mem, p, tm=512):
    n = mem.shape[0]
    row = lambda w: pl.BlockSpec((tm, w), lambda i: (i, 0))
    return pl.pallas_call(
        _memkv_kernel,
        grid=(n // tm,),
        in_specs=[row(D_MODEL), _const_spec((1, D_MODEL)), _const_spec((D_MODEL, 2 * WIDTH_M)),
                  _const_spec((WIDTH_M, WIDTH_M)), _const_spec((1, WIDTH_M))],
        out_specs=[row(WIDTH_M)] * 4,
        out_shape=[jax.ShapeDtypeStruct((n, WIDTH_M), dt) for dt in (F32, BF16, F32, BF16)],
        compiler_params=_cparams(("parallel",)),
        name="memkv",
    )(mem, p["mem_norm_g"], p["w_mem_kv"], p["bd128"], p["mk_norm_g"])


def _half_masks():
    lane = lax.broadcasted_iota(jnp.int32, (1, LANES), 1)
    return lane < HEAD_DIM_A, lane >= HEAD_DIM_A


def _mem_attend(qm_ref, mk, mv, m_ref):
    for h in range(N_HEADS_M):
        sl = slice(h * HEAD_DIM_M, (h + 1) * HEAD_DIM_M)
        s = _dot_nt(qm_ref[:, sl], mk[:, sl]) * HEAD_DIM_M ** -0.5
        e = jnp.exp(s - jnp.max(s, axis=1, keepdims=True))
        o = _dot(e.astype(BF16), mv[:, sl])
        m_ref[:, sl] = (o * (1.0 / jnp.sum(e, axis=1, keepdims=True))).astype(m_ref.dtype)


def _pattn_kernel(q_ref, qi_ref, wi_ref, qm_ref, kb_ref, vb_ref, ki2_ref, mk_ref, mv_ref,
                  a_ref, m_ref, bias_ref, *, tq, nq, k_sel):
    halves = _half_masks()
    zero = jnp.zeros((), BF16)

    def q_block(c):
        s_len = (c + 1) * tq
        bias = bias_ref.at[:, 0:s_len]
        qpos = c * tq + lax.broadcasted_iota(jnp.int32, (tq, s_len), 0)
        kpos = lax.broadcasted_iota(jnp.int32, (tq, s_len), 1)
        if s_len <= k_sel:
            bias[...] = jnp.where(kpos <= qpos, 0.0, NEG_INF)
        else:
            ki2 = ki2_ref[0:s_len, :]
            score = jnp.zeros((tq, s_len), F32)
            for hp in range(N_IDX_HEADS // 2):
                qp = qi_ref[:, hp * LANES:(hp + 1) * LANES]
                for half in range(2):
                    h = 2 * hp + half
                    s = _dot_nt(jnp.where(halves[half], qp, zero), ki2)
                    score = score + jnp.maximum(s, 0.0) * wi_ref[:, h:h + 1]
            _select_bias(jnp.where(kpos <= qpos, score, NEG_INF), k_sel, bias)

        for hp in range(N_HEADS_A // 2):
            sl = slice(hp * LANES, (hp + 1) * LANES)
            qp, kp, vp = q_ref[:, sl], kb_ref[0:s_len, sl], vb_ref[0:s_len, sl]
            acc = jnp.zeros((tq, LANES), F32)
            for half in range(2):
                s = _dot_nt(jnp.where(halves[half], qp, zero), kp) + bias[...]
                e = jnp.exp(s - jnp.max(s, axis=1, keepdims=True))
                o = _dot(e.astype(BF16), jnp.where(halves[half], vp, zero))
                acc = acc + o * (1.0 / jnp.sum(e, axis=1, keepdims=True))
            a_ref[:, sl] = acc.astype(BF16)

    i = pl.program_id(1)
    for c in range(nq):
        pl.when(i == c)(functools.partial(q_block, c))

    _mem_attend(qm_ref, mk_ref[...], mv_ref[...], m_ref)


def _prompt_attention(z, mkb, mvb, bsz, seq, tq=256):
    nq = seq // tq
    k_sel = min(TOPK_MAX, seq // 4)
    qrow = lambda w: pl.BlockSpec((tq, w), lambda b, i: (b * nq + i, 0))
    brow = lambda rows, w: pl.BlockSpec((rows, w), lambda b, i: (b, 0))
    n = bsz * seq
    return pl.pallas_call(
        functools.partial(_pattn_kernel, tq=tq, nq=nq, k_sel=k_sel),
        grid=(bsz, nq),
        in_specs=[qrow(WIDTH_A), qrow(WIDTH_A), qrow(LANES), qrow(WIDTH_M),
                  brow(seq, WIDTH_A), brow(seq, WIDTH_A), brow(seq, LANES),
                  brow(N_MEM, WIDTH_M), brow(N_MEM, WIDTH_M)],
        out_specs=[qrow(WIDTH_A), qrow(WIDTH_M)],
        out_shape=[jax.ShapeDtypeStruct((n, WIDTH_A), BF16), jax.ShapeDtypeStruct((n, WIDTH_M), BF16)],
        scratch_shapes=[pltpu.VMEM((tq, seq), F32)],
        compiler_params=_cparams(("parallel", "arbitrary")),
        name="prompt_attn",
    )(z["q"], z["qi"], z["wi"], z["qm"], z["kb"], z["vb"], z["ki2"], mkb, mvb)


def _sidx_kernel(pt_ref, qb_ref, wi_ref, kin_ref, *rest, n_pages, t_new):
    page_refs, out_ref = rest[:n_pages], rest[n_pages]
    qb = qb_ref[...]
    wcol = jnp.concatenate([wi_ref[:, h:h + 1] for h in range(N_IDX_HEADS)], axis=0)

    def head_sum(s):
        r = jnp.maximum(s, 0.0) * wcol
        sc = r[0:t_new]
        for h in range(1, N_IDX_HEADS):
            sc = sc + r[h * t_new:(h + 1) * t_new]
        return sc

    for j in range(n_pages):
        out_ref[:, j * PAGE_SIZE:(j + 1) * PAGE_SIZE] = head_sum(_dot(qb, page_refs[j][...].astype(BF16)))
    knew = jnp.concatenate([kin_ref[...], jnp.zeros((PAGE_SIZE - t_new, IDX_DIM), F32)], axis=0)
    sc = head_sum(_dot_nt(qb, knew.astype(BF16)))
    tpos = lax.broadcasted_iota(jnp.int32, (t_new, PAGE_SIZE), 0)
    cpos = lax.broadcasted_iota(jnp.int32, (t_new, PAGE_SIZE), 1)
    out_ref[:, n_pages * PAGE_SIZE:] = jnp.where(cpos <= tpos, sc, NEG_INF)


def _select_kernel(score_ref, bias_ref, *, k_sel):
    _select_bias(score_ref[...], k_sel, bias_ref)


def _sattn_kernel(pt_ref, q_ref, bias_ref, kn_ref, vn_ref, qmb_ref, mk_ref, mv_ref, *rest,
                  n_pages, t_new):
    k_refs, v_refs = rest[:n_pages], rest[n_pages:2 * n_pages]
    a_ref, m_ref = rest[2 * n_pages:]

    rows = N_HEADS_A * t_new
    r_h = lax.broadcasted_iota(jnp.int32, (rows, WIDTH_A), 0) // t_new
    l_h = lax.broadcasted_iota(jnp.int32, (rows, WIDTH_A), 1) // HEAD_DIM_A
    keep = r_h == l_h
    qblk = jnp.where(keep, jnp.tile(q_ref[...], (N_HEADS_A, 1)), 0.0).astype(BF16)
    bias = jnp.tile(bias_ref[...], (N_HEADS_A, 1))
    pad = jnp.zeros((PAGE_SIZE - t_new, WIDTH_A), F32)
    knew = jnp.concatenate([kn_ref[...], pad], axis=0).astype(BF16)
    vnew = jnp.concatenate([vn_ref[...], pad], axis=0).astype(BF16)
    chunks = [_dot(qblk, k_refs[j][...].astype(BF16)) for j in range(n_pages)]
    chunks.append(_dot_nt(qblk, knew))
    s = jnp.concatenate(chunks, axis=1) + bias
    e = jnp.exp(s - jnp.max(s, axis=1, keepdims=True))
    linv = 1.0 / jnp.sum(e, axis=1, keepdims=True)
    eb = e.astype(BF16)
    o = _dot(eb[:, n_pages * PAGE_SIZE:], vnew)
    for j in range(n_pages):
        o = o + _dot_nt(eb[:, j * PAGE_SIZE:(j + 1) * PAGE_SIZE], v_refs[j][...].astype(BF16))
    o = jnp.where(keep, o * linv, 0.0)
    out = o[0:t_new]
    for h in range(1, N_HEADS_A):
        out = out + o[h * t_new:(h + 1) * t_new]
    a_ref[...] = out

    rows_m = N_HEADS_M * t_new
    n_col = N_MEM * N_HEADS_M
    row_h = lax.broadcasted_iota(jnp.int32, (rows_m, n_col), 0) // t_new
    col_h = lax.broadcasted_iota(jnp.int32, (rows_m, n_col), 1) % N_HEADS_M
    sm = _dot_nt(qmb_ref[...], mk_ref[...].astype(BF16)) * HEAD_DIM_M ** -0.5
    sm = jnp.where(row_h == col_h, sm, NEG_INF)
    em = jnp.exp(sm - jnp.max(sm, axis=1, keepdims=True))
    om = _dot(em.astype(BF16), mv_ref[...].astype(BF16)) * (1.0 / jnp.sum(em, axis=1, keepdims=True))
    m_ref[...] = jnp.concatenate([om[h * t_new:(h + 1) * t_new] for h in range(N_HEADS_M)], axis=1)


def _sample_attention(z, layer, page_table, cache_kt, cache_vt, cache_kidxt, cache_mem_k2, cache_mem_v2,
                      db, t_new):
    n_pages = page_table.shape[1]
    past = n_pages * PAGE_SIZE
    s_pad = past + PAGE_SIZE
    k_sel = min(TOPK_MAX, (past + t_new) // 4)
    n = db * t_new

    def head_major(a, n_heads, head_dim):
        return a.reshape(db, t_new, n_heads, head_dim).transpose(0, 2, 1, 3).reshape(
            db, n_heads * t_new, head_dim)

    brow = lambda w: pl.BlockSpec((t_new, w), lambda b, pt: (b, 0))

    def page_spec(rows, j):
        return pl.BlockSpec((None, None, rows, PAGE_SIZE), lambda b, pt: (layer, pt[b, j], 0, 0))

    score = pl.pallas_call(
        functools.partial(_sidx_kernel, n_pages=n_pages, t_new=t_new),
        grid_spec=pltpu.PrefetchScalarGridSpec(
            num_scalar_prefetch=1, grid=(db,),
            in_specs=[pl.BlockSpec((None, N_IDX_HEADS * t_new, IDX_DIM), lambda b, pt: (b, 0, 0)),
                      brow(LANES), brow(IDX_DIM)] + [page_spec(IDX_DIM, j) for j in range(n_pages)],
            out_specs=brow(s_pad)),
        out_shape=jax.ShapeDtypeStruct((n, s_pad), F32),
        compiler_params=_cparams(("parallel",)),
        name="sample_index",
    )(page_table, head_major(z["qi"], N_IDX_HEADS, IDX_DIM), z["wi"], z["ki"], *([cache_kidxt] * n_pages))

    tr = min(256, n)
    bias = pl.pallas_call(
        functools.partial(_select_kernel, k_sel=k_sel),
        grid=(n // tr,),
        in_specs=[pl.BlockSpec((tr, s_pad), lambda i: (i, 0))],
        out_specs=pl.BlockSpec((tr, s_pad), lambda i: (i, 0)),
        out_shape=jax.ShapeDtypeStruct((n, s_pad), F32),
        compiler_params=_cparams(("parallel",)),
        name="sample_select",
    )(score)

    mem_spec = pl.BlockSpec((None, None, N_MEM * N_HEADS_M, HEAD_DIM_M), lambda b, pt: (layer, b, 0, 0))
    return pl.pallas_call(
        functools.partial(_sattn_kernel, n_pages=n_pages, t_new=t_new),
        grid_spec=pltpu.PrefetchScalarGridSpec(
            num_scalar_prefetch=1, grid=(db,),
            in_specs=[brow(WIDTH_A), brow(s_pad), brow(WIDTH_A), brow(WIDTH_A),
                      pl.BlockSpec((None, N_HEADS_M * t_new, HEAD_DIM_M), lambda b, pt: (b, 0, 0)),
                      mem_spec, mem_spec]
                     + [page_spec(WIDTH_A, j) for j in range(n_pages)] * 2,
            out_specs=[brow(WIDTH_A), brow(WIDTH_M)]),
        out_shape=[jax.ShapeDtypeStruct((n, WIDTH_A), F32), jax.ShapeDtypeStruct((n, WIDTH_M), F32)],
        compiler_params=_cparams(("parallel",)),
        name="sample_attn",
    )(page_table, z["q"].astype(F32), bias, z["k"], z["v"], head_major(z["qm"], N_HEADS_M, HEAD_DIM_M),
      cache_mem_k2, cache_mem_v2, *([cache_kt] * n_pages), *([cache_vt] * n_pages))


def _lru_gates(xc, wa_ref, ba_ref, wx_ref, bx_ref, lam_ref):
    xb = xc.astype(BF16)
    r = jax.nn.sigmoid(_dot(xb, wa_ref[...]) + ba_ref[...])
    ig = jax.nn.sigmoid(_dot(xb, wx_ref[...]) + bx_ref[...])
    log_a = -LRU_C * r * _softplus(-lam_ref[...])
    a = jnp.exp(log_a)
    u = jnp.sqrt(-jnp.tanh(log_a) * (a * a + 1.0)) * ig * xc
    return a, u


def _rnn_prompt_kernel(xr_ref, gr_ref, cw_ref, cb_ref, wa_ref, ba_ref, wx_ref, bx_ref, lam_ref,
                       r_ref, h_ref, buf_ref, prev_s, hc_s, *, tt):
    @pl.when(pl.program_id(1) == 0)
    def _():
        prev_s[...] = jnp.zeros_like(prev_s)
        hc_s[...] = jnp.zeros_like(hc_s)

    x = xr_ref[...]
    ext = jnp.concatenate([prev_s[...], x], axis=0)
    conv = cb_ref[...] + cw_ref[RNN_CONV - 1:RNN_CONV] * ext
    for j in range(1, RNN_CONV):
        conv = conv + cw_ref[RNN_CONV - 1 - j:RNN_CONV - j] * pltpu.roll(ext, j, 0)
    xc = conv[SUBLANES:]
    a, u = _lru_gates(xc, wa_ref, ba_ref, wx_ref, bx_ref, lam_ref)

    row = lax.broadcasted_iota(jnp.int32, (tt, 1), 0)
    step = 1
    while step < tt:
        m = row >= step
        u = jnp.where(m, a * pltpu.roll(u, step, 0) + u, u)
        a = jnp.where(m, a * pltpu.roll(a, step, 0), a)
        step *= 2
    h = u + a * hc_s[0:1]
    r_ref[...] = (h * _gelu(gr_ref[...])).astype(BF16)
    h_tail = h[tt - SUBLANES:]
    h_ref[...] = h_tail
    hc_s[...] = jnp.broadcast_to(h_tail[SUBLANES - 1:SUBLANES], hc_s.shape)
    buf_ref[...] = x[tt - SUBLANES:]
    prev_s[...] = x[tt - SUBLANES:]


def _rnn_prompt(xr, gr, p, bsz, seq, tt=256):
    nt = seq // tt
    row = pl.BlockSpec((tt, WIDTH_RNN), lambda b, i: (b * nt + i, 0))
    tail = pl.BlockSpec((None, SUBLANES, WIDTH_RNN), lambda b, i: (b, 0, 0))
    vec = _const_spec((1, WIDTH_RNN))
    sq = _const_spec((WIDTH_RNN, WIDTH_RNN))
    return pl.pallas_call(
        functools.partial(_rnn_prompt_kernel, tt=tt),
        grid=(bsz, nt),
        in_specs=[row, row, _const_spec((RNN_CONV, WIDTH_RNN)), vec, sq, vec, sq, vec, vec],
        out_specs=[row, tail, tail],
        out_shape=[jax.ShapeDtypeStruct((bsz * seq, WIDTH_RNN), BF16),
                   jax.ShapeDtypeStruct((bsz, SUBLANES, WIDTH_RNN), F32),
                   jax.ShapeDtypeStruct((bsz, SUBLANES, WIDTH_RNN), F32)],
        scratch_shapes=[pltpu.VMEM((SUBLANES, WIDTH_RNN), F32), pltpu.VMEM((SUBLANES, WIDTH_RNN), F32)],
        compiler_params=_cparams(("parallel", "arbitrary")),
        name="rnn_prompt",
    )(xr, gr, p["rnn_conv_w"], p["rnn_conv_b"], p["rnn_wa"], p["rnn_ba"], p["rnn_wx"], p["rnn_bx"],
      p["rnn_lambda"])


def _rnn_sample_kernel(xr_ref, gr_ref, cbuf_ref, h0_ref, cw_ref, cb_ref, wa_ref, ba_ref, wx_ref, bx_ref,
                       lam_ref, r_ref, h_ref, buf_ref, *, db, t_new):
    x = xr_ref[...]
    n = db * t_new
    ext = jnp.concatenate([cbuf_ref[...], x], axis=0)
    conv = cb_ref[...] + cw_ref[0:1] * ext[0:n]
    for j in range(1, RNN_CONV):
        conv = conv + cw_ref[j:j + 1] * ext[j * db:j * db + n]
    a, u = _lru_gates(conv, wa_ref, ba_ref, wx_ref, bx_ref, lam_ref)
    h = h0_ref[...]
    hs = []
    for t in range(t_new):
        h = a[t * db:(t + 1) * db] * h + u[t * db:(t + 1) * db]
        hs.append(h)
    r_ref[...] = jnp.concatenate(hs, axis=0) * _gelu(gr_ref[...])
    h_ref[...] = h
    buf_ref[...] = ext[n:]


def _rnn_sample(xr_tm, gr_tm, cbuf_tm, h0, p, db, t_new):
    n = db * t_new
    nb = (RNN_CONV - 1) * db
    full = lambda r, c: _const_spec((r, c))
    vec = full(1, WIDTH_RNN)
    sq = full(WIDTH_RNN, WIDTH_RNN)
    return pl.pallas_call(
        functools.partial(_rnn_sample_kernel, db=db, t_new=t_new),
        grid=(1,),
        in_specs=[full(n, WIDTH_RNN), full(n, WIDTH_RNN), full(nb, WIDTH_RNN), full(db, WIDTH_RNN),
                  full(RNN_CONV, WIDTH_RNN), vec, sq, vec, sq, vec, vec],
        out_specs=[full(n, WIDTH_RNN), full(db, WIDTH_RNN), full(nb, WIDTH_RNN)],
        out_shape=[jax.ShapeDtypeStruct((n, WIDTH_RNN), F32), jax.ShapeDtypeStruct((db, WIDTH_RNN), F32),
                   jax.ShapeDtypeStruct((nb, WIDTH_RNN), F32)],
        compiler_params=_cparams(("arbitrary",)),
        name="rnn_sample",
    )(xr_tm, gr_tm, cbuf_tm, h0, p["rnn_conv_w"], p["rnn_conv_b"], p["rnn_wa"], p["rnn_ba"],
      p["rnn_wx"], p["rnn_bx"], p["rnn_lambda"])


def _merge_kernel(x_ref, a_ref, r_ref, m_ref, gate_ref, wb_ref, wo_ref, out_ref):
    y = None
    for n, br in enumerate((a_ref, r_ref, m_ref)):
        proj = _dot(br[...].astype(BF16), wb_ref[n])
        t = gate_ref[:, n * D_MODEL:(n + 1) * D_MODEL].astype(F32) * proj
        y = t if y is None else y + t
    out_ref[...] = x_ref[...] + _dot(y.astype(BF16), wo_ref[...])


def _merge(x, a, r, m, gate, p, tm):
    n = x.shape[0]
    row = lambda w: pl.BlockSpec((tm, w), lambda i: (i, 0))
    return pl.pallas_call(
        _merge_kernel,
        grid=(n // tm,),
        in_specs=[row(D_MODEL), row(WIDTH_A), row(WIDTH_RNN), row(WIDTH_M), row(N_BRANCH * D_MODEL),
                  _const_spec((N_BRANCH, WIDTH_A, D_MODEL)), _const_spec((D_MODEL, D_MODEL))],
        out_specs=row(D_MODEL),
        out_shape=jax.ShapeDtypeStruct((n, D_MODEL), F32),
        compiler_params=_cparams(("parallel",)),
        name="merge",
    )(x, a, r, m, gate, p["w_branch"], p["w_out"])


def _ffn_kernel(*refs, tm, stride, n_prev, use_state):
    if use_state:
        x_ref, g_ref, wg_ref, wv_ref, wd_ref, cw_ref, cb_ref, st_ref, out_ref, fb_ref, xn_s = refs
    else:
        x_ref, g_ref, wg_ref, wv_ref, wd_ref, cw_ref, cb_ref, out_ref, fb_ref, xn_s, carry_s = refs
    i = pl.program_id(1)
    j = pl.program_id(2)

    @pl.when(j == 0)
    def _():
        x = x_ref[...]
        xn_s[...] = _rmsnorm_rows(x, g_ref[...]).astype(BF16)
        out_ref[...] = x

    xn = xn_s[...]
    gt = _dot(xn, wg_ref[...])
    val = _dot(xn, wv_ref[...])
    if use_state:
        prev = st_ref[...]
    else:
        @pl.when(i == 0)
        def _():
            carry_s[j] = jnp.zeros((SUBLANES, gt.shape[1]), F32)

        prev = carry_s[j]
    ext = jnp.concatenate([prev, gt], axis=0)
    conv = cb_ref[...] + cw_ref[FFN_CONV - 1:FFN_CONV] * ext
    for s in range(1, FFN_CONV):
        conv = conv + cw_ref[FFN_CONV - 1 - s:FFN_CONV - s] * pltpu.roll(ext, s * stride, 0)
    act = _gelu(conv[n_prev:]) * val
    out_ref[...] += _dot(act.astype(BF16), wd_ref[...])
    fb_ref[...] = gt[tm - n_prev:]
    if not use_state:
        carry_s[j] = gt[tm - n_prev:]


def _ffn(x, p, *, groups, rows_per_group, tm, stride, state=None, tf=512):
    nt = rows_per_group // tm
    nj = D_FF // tf
    use_state = state is not None
    n_prev = (FFN_CONV - 1) * stride if use_state else SUBLANES
    n = groups * rows_per_group
    xrow = pl.BlockSpec((tm, D_MODEL), lambda b, i, j: (b * nt + i, 0))
    in_specs = [xrow, _const_spec((1, D_MODEL)),
                pl.BlockSpec((D_MODEL, tf), lambda b, i, j: (0, j)),
                pl.BlockSpec((D_MODEL, tf), lambda b, i, j: (0, nj + j)),
                pl.BlockSpec((tf, D_MODEL), lambda b, i, j: (j, 0)),
                pl.BlockSpec((FFN_CONV, tf), lambda b, i, j: (0, j)),
                pl.BlockSpec((1, tf), lambda b, i, j: (0, j))]
    args = [x, p["norm_ffn_g"], p["w_ffn_up"], p["w_ffn_up"], p["w_ffn_down"], p["ffn_conv_w"],
            p["ffn_conv_b"]]
    scratch = [pltpu.VMEM((tm, D_MODEL), BF16)]
    if use_state:
        in_specs.append(pl.BlockSpec((n_prev, tf), lambda b, i, j: (0, j)))
        args.append(state)
    else:
        scratch.append(pltpu.VMEM((nj, SUBLANES, tf), F32))
    return pl.pallas_call(
        functools.partial(_ffn_kernel, tm=tm, stride=stride, n_prev=n_prev, use_state=use_state),
        grid=(groups, nt, nj),
        in_specs=in_specs,
        out_specs=[xrow, pl.BlockSpec((None, n_prev, tf), lambda b, i, j: (b * nt + i, 0, j))],
        out_shape=[jax.ShapeDtypeStruct((n, D_MODEL), F32),
                   jax.ShapeDtypeStruct((groups * nt, n_prev, D_FF), F32)],
        scratch_shapes=scratch,
        compiler_params=_cparams(("parallel", "arbitrary", "arbitrary")),
        name="ffn",
    )(*args)


def _rope_tables(pos):
    half = ROPE_DIM // 2
    freqs = jnp.power(ROPE_THETA, -jnp.arange(half, dtype=F32) * 2.0 / ROPE_DIM)
    ang = pos.astype(F32)[:, None] * freqs[None, :]
    d = jnp.arange(LANES) % HEAD_DIM_A
    cos = jnp.cos(ang)[:, d % half]
    sin = jnp.sin(ang)[:, d % half]
    c = jnp.where(d < ROPE_DIM, cos, 1.0)
    s1 = jnp.where(d < half, -sin, 0.0)
    s2 = jnp.where((d >= half) & (d < ROPE_DIM), sin, 0.0)
    return c, s1, s2


def _block_diag(w):
    nl, nb, c, _ = w.shape
    eye = jnp.eye(nb, dtype=w.dtype)
    return (w[:, :, :, None, :] * eye[None, :, None, :, None]).reshape(nl, nb * c, nb * c)


def _prep_params(norm_mix_g, w_in, q_norm_g, k_norm_g, mq_norm_g, mk_norm_g, mem_norm_g, w_mem_kv,
                 rnn_conv_w, rnn_conv_b, rnn_wa, rnn_ba, rnn_wx, rnn_bx, rnn_lambda,
                 w_branch, w_out, norm_ffn_g, w_ffn_up, ffn_conv_w, ffn_conv_b, w_ffn_down):
    q, k, v, qi, ki, wi, xr, gr, qm, gates = jnp.split(w_in, IN_SPLITS, axis=-1)
    wi_p = jnp.pad(wi, ((0, 0), (0, 0), (0, LANES - N_IDX_HEADS)))
    w_in_p = jnp.concatenate([q, k, v, qi, ki, ki, wi_p, xr, gr, qm, gates], axis=-1).astype(BF16)
    row = lambda a: a[:, None, :]
    stacked = {
        "norm_mix_g": row(norm_mix_g), "w_in": w_in_p,
        "q_norm_g": row(jnp.tile(q_norm_g, (1, N_HEADS_A))), "k_norm_g": row(jnp.tile(k_norm_g, (1, N_HEADS_A))),
        "mq_norm_g": row(jnp.tile(mq_norm_g, (1, N_HEADS_M))), "mk_norm_g": row(jnp.tile(mk_norm_g, (1, N_HEADS_M))),
        "mem_norm_g": row(mem_norm_g), "w_mem_kv": w_mem_kv.astype(BF16),
        "rnn_conv_w": rnn_conv_w, "rnn_conv_b": row(rnn_conv_b),
        "rnn_wa": _block_diag(rnn_wa).astype(BF16), "rnn_ba": row(rnn_ba),
        "rnn_wx": _block_diag(rnn_wx).astype(BF16), "rnn_bx": row(rnn_bx),
        "rnn_lambda": row(rnn_lambda),
        "w_branch": w_branch.astype(BF16), "w_out": w_out.astype(BF16),
        "norm_ffn_g": row(norm_ffn_g), "w_ffn_up": w_ffn_up.astype(BF16),
        "ffn_conv_w": ffn_conv_w, "ffn_conv_b": row(ffn_conv_b), "w_ffn_down": w_ffn_down.astype(BF16),
    }
    ones = lambda n, c: jnp.kron(jnp.eye(n, dtype=F32), jnp.ones((c, c), F32)).astype(BF16)
    shared = {"bd64": ones(N_HEADS_A, HEAD_DIM_A), "bd128": ones(N_HEADS_M, HEAD_DIM_M)}
    return [dict({k_: v_[l] for k_, v_ in stacked.items()}, **shared) for l in range(DEPTH)]


def kernel(x_prompt, x_sample, mem_prompt, cache_k, cache_v, cache_kidx, page_table, cache_mem_k, cache_mem_v, state_rnn_h, state_rnn_conv, state_ffn_conv, norm_mix_g, w_in, q_norm_g, k_norm_g, mq_norm_g, mk_norm_g, mem_norm_g, w_mem_kv, rnn_conv_w, rnn_conv_b, rnn_wa, rnn_ba, rnn_wx, rnn_bx, rnn_lambda, w_branch, w_out, norm_ffn_g, w_ffn_up, ffn_conv_w, ffn_conv_b, w_ffn_down):
    bsz, seq, _ = x_prompt.shape
    db, t_new, _ = x_sample.shape
    n_pool = cache_k.shape[1]
    past = page_table.shape[1] * PAGE_SIZE
    params = _prep_params(norm_mix_g, w_in, q_norm_g, k_norm_g, mq_norm_g, mk_norm_g, mem_norm_g, w_mem_kv,
                          rnn_conv_w, rnn_conv_b, rnn_wa, rnn_ba, rnn_wx, rnn_bx, rnn_lambda,
                          w_branch, w_out, norm_ffn_g, w_ffn_up, ffn_conv_w, ffn_conv_b, w_ffn_down)
    tabs_p = _rope_tables(jnp.tile(jnp.arange(seq), bsz))
    tabs_s = _rope_tables(jnp.tile(past + jnp.arange(t_new), db))
    cache_kt = cache_k.transpose(0, 1, 3, 4, 2).reshape(DEPTH, n_pool, WIDTH_A, PAGE_SIZE)
    cache_vt = cache_v.transpose(0, 1, 3, 4, 2).reshape(DEPTH, n_pool, WIDTH_A, PAGE_SIZE)
    cache_kidxt = cache_kidx.transpose(0, 1, 3, 2)
    cache_mem_k2 = cache_mem_k.reshape(DEPTH, db, N_MEM * N_HEADS_M, HEAD_DIM_M)
    cache_mem_v2 = cache_mem_v.reshape(DEPTH, db, N_MEM * N_HEADS_M, HEAD_DIM_M)
    mem = mem_prompt.reshape(bsz * N_MEM, D_MODEL)

    def to_tm(a):
        return a.reshape(db, t_new, -1).transpose(1, 0, 2).reshape(db * t_new, -1)

    def to_bm(a):
        return a.reshape(t_new, db, -1).transpose(1, 0, 2).reshape(db * t_new, -1)

    xp = x_prompt.reshape(bsz * seq, D_MODEL)
    xs = x_sample.reshape(db * t_new, D_MODEL)
    outs = [[] for _ in range(14)]
    tm_p = min(512, seq)
    tm_s = min(512, db * t_new)
    for l in range(DEPTH):
        p = params[l]
        mk, mkb, mv, mvb = _memkv(mem, p)
        zp = _inproj(xp, p, tabs_p, tm=tm_p)
        a_p, m_p = _prompt_attention(zp, mkb, mvb, bsz, seq)
        r_p, h_p, rb_p = _rnn_prompt(zp["xr"], zp["gr"], p, bsz, seq)
        xp = _merge(xp, a_p, r_p, m_p, zp["gate"], p, tm=tm_p)
        xp, fb_p = _ffn(xp, p, groups=bsz, rows_per_group=seq, tm=tm_p, stride=1)
        zs = _inproj(xs, p, tabs_s, tm=tm_s)
        a_s, m_s = _sample_attention(zs, l, page_table, cache_kt, cache_vt, cache_kidxt,
                                     cache_mem_k2, cache_mem_v2, db, t_new)
        cbuf_tm = state_rnn_conv[l].transpose(1, 0, 2).reshape((RNN_CONV - 1) * db, WIDTH_RNN)
        r_s_tm, h_s, rb_s_tm = _rnn_sample(to_tm(zs["xr"]), to_tm(zs["gr"]), cbuf_tm, state_rnn_h[l],
                                            p, db, t_new)
        xs = _merge(xs, a_s, to_bm(r_s_tm), m_s, zs["gate"], p, tm=tm_s)
        fst_tm = state_ffn_conv[l].transpose(1, 0, 2).reshape((FFN_CONV - 1) * db, D_FF)
        xs_tm, fb_s_tm = _ffn(to_tm(xs), p, groups=1, rows_per_group=db * t_new, tm=db * t_new,
                              stride=db, state=fst_tm)
        xs = to_bm(xs_tm)

        new = (zp["k"].reshape(bsz, seq, N_HEADS_A, HEAD_DIM_A), zs["k"].reshape(db, t_new, N_HEADS_A, HEAD_DIM_A),
               zp["v"].reshape(bsz, seq, N_HEADS_A, HEAD_DIM_A), zs["v"].reshape(db, t_new, N_HEADS_A, HEAD_DIM_A),
               zp["ki"].reshape(bsz, seq, IDX_DIM), zs["ki"].reshape(db, t_new, IDX_DIM),
               mk.reshape(bsz, N_MEM, N_HEADS_M, HEAD_DIM_M), mv.reshape(bsz, N_MEM, N_HEADS_M, HEAD_DIM_M),
               h_p[:, SUBLANES - 1], h_s,
               rb_p[:, SUBLANES - (RNN_CONV - 1):],
               rb_s_tm.reshape(RNN_CONV - 1, db, WIDTH_RNN).transpose(1, 0, 2),
               fb_p.reshape(bsz, -1, SUBLANES, D_FF)[:, -1, SUBLANES - (FFN_CONV - 1):],
               fb_s_tm.reshape(FFN_CONV - 1, db, D_FF).transpose(1, 0, 2))
        for lst, val in zip(outs, new):
            lst.append(val)
    return (xp.reshape(bsz, seq, D_MODEL), xs.reshape(db, t_new, D_MODEL)) + tuple(jnp.stack(o) for o in outs)
```

```python
import functools

import jax
import jax.numpy as jnp
from jax import lax
from jax.experimental import pallas as pl
from jax.experimental.pallas import tpu as pltpu

F32 = jnp.float32
BF16 = jnp.bfloat16

D_MODEL = 1024
DEPTH = 4
PAGE_SIZE = 128
N_HEADS_A = 8
HEAD_DIM_A = 64
WIDTH_A = 512
N_IDX_HEADS = 8
IDX_DIM = 64
TOPK_MAX = 256
INDEX_WEIGHT_SCALE = N_IDX_HEADS ** -0.5 * IDX_DIM ** -0.5
WIDTH_RNN = 512
N_RNN_BLOCKS = 8
RNN_CONV = 4
LRU_C = 8.0
N_MEM = 256
N_HEADS_M = 4
HEAD_DIM_M = 128
WIDTH_M = 512
N_BRANCH = 3
D_FF = 3 * D_MODEL
FFN_CONV = 3
ROPE_THETA = 500000.0
ROPE_DIM = 16
EPS = 1e-6

IN_SIZES = (WIDTH_A, WIDTH_A, WIDTH_A, N_IDX_HEADS * IDX_DIM, IDX_DIM, N_IDX_HEADS,
            WIDTH_RNN, WIDTH_RNN, WIDTH_M, N_BRANCH * D_MODEL)
IN_SPLITS = tuple(sum(IN_SIZES[:i + 1]) for i in range(len(IN_SIZES) - 1))

LANES = 128
SUBLANES = 8
OFF_Q, OFF_K, OFF_V, OFF_QI = 0, 512, 1024, 1536
OFF_KI2 = 2048
OFF_WI = OFF_KI2 + LANES
OFF_XR = OFF_WI + LANES
OFF_GR = OFF_XR + WIDTH_RNN
OFF_QM = OFF_GR + WIDTH_RNN
OFF_G = OFF_QM + WIDTH_M
D_IN_P = OFF_G + N_BRANCH * D_MODEL

VMEM_LIMIT = 56 * 1024 * 1024

NEG_INF = float("-inf")
INT_MIN = -2 ** 31
NEG_INF_KEY = INT_MIN + 0x7FFFFF
MASKED = -1e30


def _cparams(sem):
    return pltpu.CompilerParams(dimension_semantics=sem, vmem_limit_bytes=VMEM_LIMIT)


def _const_spec(shape):
    nd = len(shape)
    return pl.BlockSpec(shape, lambda *_: (0,) * nd)


def _gelu(x):
    return 0.5 * x * (1.0 + jnp.tanh(0.7978845608028654 * (x + 0.044715 * (x * x * x))))


def _softplus(z):
    return jnp.maximum(z, 0.0) + jnp.log1p(jnp.exp(-jnp.abs(z)))


def _rmsnorm_rows(x, g):
    return x * lax.rsqrt(jnp.mean(x * x, axis=-1, keepdims=True) + EPS) * g


def _dot(a, b):
    return jnp.dot(a, b, preferred_element_type=F32)


def _dot_nt(a, b):
    return lax.dot_general(a, b, (((1,), (1,)), ((), ())), preferred_element_type=F32)


def _order_key(score):
    score = jnp.where(score == 0.0, 0.0, score)
    bits = lax.bitcast_convert_type(score, jnp.int32)
    return jnp.where(bits < 0, bits ^ jnp.int32(0x7FFFFFFF), bits)


def _select_bias(score, k_sel, bias_ref):
    rows, s_len = score.shape
    valid = score > NEG_INF
    score = jnp.where(score == 0.0, 0.0, score)
    bits = lax.bitcast_convert_type(score, jnp.int32)
    key = jnp.where(bits < 0, bits ^ jnp.int32(0x7FFFFFFF), bits)
    kf = jnp.float32(k_sel)

    def count_ge(c):
        return jnp.sum(jnp.where(key >= c, 1.0, 0.0), axis=1, keepdims=True)

    thr = jnp.where(count_ge(jnp.int32(0)) >= kf, jnp.int32(0), jnp.int32(INT_MIN))

    def body(it, thr):
        cand = thr + jnp.left_shift(jnp.int32(1), 30 - it)
        return jnp.where(count_ge(cand) >= kf, cand, thr)

    thr = lax.fori_loop(0, 31, body, thr)
    n_ge = count_ge(thr)
    bias_ref[...] = jnp.where((key >= thr) & valid, 0.0, NEG_INF)

    tie = jnp.where((n_ge > kf) & (thr > NEG_INF_KEY), 1.0, 0.0)

    @pl.when(jnp.max(tie) > 0.0)
    def _():
        gt = jnp.where((key > thr) & valid, 1.0, 0.0)
        eq = jnp.where((key == thr) & valid, 1.0, 0.0)
        need = kf - jnp.sum(gt, axis=1, keepdims=True)
        r_i = lax.broadcasted_iota(jnp.int32, (LANES, LANES), 0)
        c_i = lax.broadcasted_iota(jnp.int32, (LANES, LANES), 1)
        tri = jnp.where(r_i < c_i, 1.0, 0.0).astype(BF16)
        run = jnp.zeros((rows, 1), F32)
        for c in range(s_len // LANES):
            sl = slice(c * LANES, (c + 1) * LANES)
            eq_c = eq[:, sl]
            before = _dot(eq_c.astype(BF16), tri) + run
            sel = (gt[:, sl] > 0.0) | ((eq_c > 0.0) & (before < need))
            bias_ref[:, sl] = jnp.where(sel, 0.0, NEG_INF)
            run = run + jnp.sum(eq_c, axis=1, keepdims=True)


def _inproj_kernel(x_ref, g_ref, w_ref, bd64_ref, bd128_ref, qg_ref, kg_ref, mqg_ref,
                   c_ref, s1_ref, s2_ref,
                   q_ref, qx_ref, k_ref, kb_ref, v_ref, vx_ref, qi_ref, qix_ref, ki_ref, ki2_ref, wi_ref,
                   xr_ref, gr_ref, qm_ref, gate_ref):
    xn = _rmsnorm_rows(x_ref[...], g_ref[...]).astype(BF16)
    lo, hi = _half_masks()

    def expand_heads(z):
        parts = []
        for hp in range(z.shape[1] // LANES):
            pair = z[:, hp * LANES:(hp + 1) * LANES]
            parts += [jnp.where(lo, pair, 0.0), jnp.where(hi, pair, 0.0)]
        return jnp.concatenate(parts, axis=1).astype(BF16)

    def seg(off, width):
        return _dot(xn, w_ref[:, off:off + width])

    c1, s1, s2 = c_ref[...], s1_ref[...], s2_ref[...]

    def rope(z):
        w = z.shape[1]
        n = w // LANES
        cc, a, b = (jnp.tile(t, (1, n)) if n > 1 else t for t in (c1, s1, s2))
        return z * cc + pltpu.roll(z, w - ROPE_DIM // 2, 1) * a + pltpu.roll(z, ROPE_DIM // 2, 1) * b

    def headnorm(z, bd_ref, dim, g):
        ss = _dot((z * z).astype(BF16), bd_ref[...]) * (1.0 / dim)
        return z * lax.rsqrt(ss + EPS) * g

    q = rope(headnorm(seg(OFF_Q, WIDTH_A), bd64_ref, HEAD_DIM_A, qg_ref[...]))
    q = q * HEAD_DIM_A ** -0.5
    q_ref[...] = q.astype(BF16)
    qx_ref[...] = expand_heads(q)
    k = rope(headnorm(seg(OFF_K, WIDTH_A), bd64_ref, HEAD_DIM_A, kg_ref[...]))
    k_ref[...] = k
    kb_ref[...] = k.astype(BF16)
    v = seg(OFF_V, WIDTH_A)
    v_ref[...] = v
    vx_ref[...] = expand_heads(v)
    qi = rope(seg(OFF_QI, WIDTH_A))
    qi_ref[...] = qi.astype(BF16)
    qix_ref[...] = expand_heads(qi)
    ki2 = rope(seg(OFF_KI2, LANES))
    ki2_ref[...] = ki2.astype(BF16)
    ki_ref[...] = ki2[:, :IDX_DIM]
    wi_ref[...] = seg(OFF_WI, LANES) * INDEX_WEIGHT_SCALE
    xr_ref[...] = seg(OFF_XR, WIDTH_RNN)
    gr_ref[...] = seg(OFF_GR, WIDTH_RNN)
    qm_ref[...] = headnorm(seg(OFF_QM, WIDTH_M), bd128_ref, HEAD_DIM_M, mqg_ref[...]).astype(BF16)
    for c in range(N_BRANCH * D_MODEL // 512):
        gate_ref[:, c * 512:(c + 1) * 512] = jax.nn.sigmoid(seg(OFF_G + c * 512, 512)).astype(BF16)


def _inproj(x, p, rope_tabs, tm):
    n = x.shape[0]
    row = lambda w: pl.BlockSpec((tm, w), lambda i: (i, 0))
    outs = [("q", WIDTH_A, BF16), ("qx", 2 * WIDTH_A, BF16), ("k", WIDTH_A, F32), ("kb", WIDTH_A, BF16),
            ("v", WIDTH_A, F32), ("vx", 2 * WIDTH_A, BF16), ("qi", WIDTH_A, BF16),
            ("qix", 2 * WIDTH_A, BF16), ("ki", IDX_DIM, F32), ("ki2", LANES, BF16),
            ("wi", LANES, F32), ("xr", WIDTH_RNN, F32), ("gr", WIDTH_RNN, F32), ("qm", WIDTH_M, BF16),
            ("gate", N_BRANCH * D_MODEL, BF16)]
    res = pl.pallas_call(
        _inproj_kernel,
        grid=(n // tm,),
        in_specs=[row(D_MODEL), _const_spec((1, D_MODEL)),
                  pl.BlockSpec((D_MODEL, D_IN_P), lambda i: (0, 0), pipeline_mode=pl.Buffered(1)),
                  _const_spec((WIDTH_A, WIDTH_A)), _const_spec((WIDTH_M, WIDTH_M)),
                  _const_spec((1, WIDTH_A)), _const_spec((1, WIDTH_A)), _const_spec((1, WIDTH_M)),
                  row(LANES), row(LANES), row(LANES)],
        out_specs=[row(w) for _, w, _ in outs],
        out_shape=[jax.ShapeDtypeStruct((n, w), dt) for _, w, dt in outs],
        compiler_params=_cparams(("parallel",)),
        name="inproj",
    )(x, p["norm_mix_g"], p["w_in"], p["bd64"], p["bd128"], p["q_norm_g"], p["k_norm_g"],
      p["mq_norm_g"], *rope_tabs)
    return {name: r for (name, _, _), r in zip(outs, res)}


def _memkv_kernel(m_ref, g_ref, w_ref, bd128_ref, kg_ref, mk_ref, mkb_ref, mv_ref, mvb_ref):
    xn = _rmsnorm_rows(m_ref[...], g_ref[...]).astype(BF16)
    zk = _dot(xn, w_ref[:, :WIDTH_M])
    ss = _dot((zk * zk).astype(BF16), bd128_ref[...]) * (1.0 / HEAD_DIM_M)
    mk = zk * lax.rsqrt(ss + EPS) * kg_ref[...]
    mk_ref[...] = mk
    mkb_ref[...] = mk.astype(BF16)
    mv = _dot(xn, w_ref[:, WIDTH_M:])
    mv_ref[...] = mv
    mvb_ref[...] = mv.astype(BF16)


def _memkv(mem, p, tm=512):
    n = mem.shape[0]
    row = lambda w: pl.BlockSpec((tm, w), lambda i: (i, 0))
    return pl.pallas_call(
        _memkv_kernel,
        grid=(n // tm,),
        in_specs=[row(D_MODEL), _const_spec((1, D_MODEL)), _const_spec((D_MODEL, 2 * WIDTH_M)),
                  _const_spec((WIDTH_M, WIDTH_M)), _const_spec((1, WIDTH_M))],
        out_specs=[row(WIDTH_M)] * 4,
        out_shape=[jax.ShapeDtypeStruct((n, WIDTH_M), dt) for dt in (F32, BF16, F32, BF16)],
        compiler_params=_cparams(("parallel",)),
        name="memkv",
    )(mem, p["mem_norm_g"], p["w_mem_kv"], p["bd128"], p["mk_norm_g"])


def _half_masks():
    lane = lax.broadcasted_iota(jnp.int32, (1, LANES), 1)
    return lane < HEAD_DIM_A, lane >= HEAD_DIM_A


def _mem_attend(qm_ref, mk, mv, m_ref):
    for h in range(N_HEADS_M):
        sl = slice(h * HEAD_DIM_M, (h + 1) * HEAD_DIM_M)
        s = _dot_nt(qm_ref[:, sl], mk[:, sl]) * HEAD_DIM_M ** -0.5
        e = jnp.exp(s - jnp.max(s, axis=1, keepdims=True))
        o = _dot(e.astype(BF16), mv[:, sl])
        m_ref[:, sl] = (o * (1.0 / jnp.sum(e, axis=1, keepdims=True))).astype(m_ref.dtype)


def _pattn_kernel(qx_ref, qix_ref, wi_ref, qm_ref, kb_ref, vx_ref, ki2_ref, mk_ref, mv_ref,
                  a_ref, m_ref, key_s, bias_s, acc_s, m_s, l_s, *, tq, k_sel):
    i = pl.program_id(1)
    n_chunks = i + 1
    kf = jnp.float32(k_sel)
    qpos = i * tq + lax.broadcasted_iota(jnp.int32, (tq, 1), 0)
    n_sub = tq // LANES

    def rows(kc):
        return pl.ds(pl.multiple_of(kc * tq, tq), tq)

    def index_chunk(kc, carry):
        ki2 = ki2_ref[rows(kc), :]
        score = jnp.zeros((tq, tq), F32)
        for h in range(N_IDX_HEADS):
            s = _dot_nt(qix_ref[:, h * LANES:(h + 1) * LANES], ki2)
            score = score + jnp.maximum(s, 0.0) * wi_ref[:, h:h + 1]
        kpos = kc * tq + lax.broadcasted_iota(jnp.int32, (1, tq), 1)
        key_s[kc] = _order_key(jnp.where(kpos <= qpos, score, NEG_INF))
        return carry

    lax.fori_loop(0, n_chunks, index_chunk, 0)

    def count(pred):
        def body(kc, acc):
            hit = jnp.where(pred(key_s[kc]), 1.0, 0.0)
            for c in range(n_sub):
                acc = acc + hit[:, c * LANES:(c + 1) * LANES]
            return acc

        acc = lax.fori_loop(0, n_chunks, body, jnp.zeros((tq, LANES), F32))
        return jnp.sum(acc, axis=1, keepdims=True)

    thr = jnp.where(count(lambda k: k >= 0) >= kf, jnp.int32(0), jnp.int32(INT_MIN))

    def bisect(it, thr):
        cand = thr + jnp.left_shift(jnp.int32(1), 30 - it)
        return jnp.where(count(lambda k: k >= cand) >= kf, cand, thr)

    thr = lax.fori_loop(0, 31, bisect, thr)
    n_ge = count(lambda k: k >= thr)

    def bias_chunk(kc, carry):
        k = key_s[kc]
        bias_s[kc] = jnp.where((k >= thr) & (k > NEG_INF_KEY), 0.0, MASKED)
        return carry

    lax.fori_loop(0, n_chunks, bias_chunk, 0)

    tie = jnp.where((n_ge > kf) & (thr > NEG_INF_KEY), 1.0, 0.0)

    @pl.when(jnp.max(tie) > 0.0)
    def _():
        need = kf - count(lambda k: k > thr)
        r_i = lax.broadcasted_iota(jnp.int32, (LANES, LANES), 0)
        c_i = lax.broadcasted_iota(jnp.int32, (LANES, LANES), 1)
        tri = jnp.where(r_i < c_i, 1.0, 0.0).astype(BF16)

        def tie_chunk(kc, run):
            k = key_s[kc]
            for c in range(n_sub):
                sl = slice(c * LANES, (c + 1) * LANES)
                eq = jnp.where((k[:, sl] == thr) & (k[:, sl] > NEG_INF_KEY), 1.0, 0.0)
                before = _dot(eq.astype(BF16), tri) + run
                sel = (k[:, sl] > thr) | ((eq > 0.0) & (before < need))
                bias_s[kc, :, sl] = jnp.where(sel, 0.0, MASKED)
                run = run + jnp.sum(eq, axis=1, keepdims=True)
            return run

        lax.fori_loop(0, n_chunks, tie_chunk, jnp.zeros((tq, 1), F32))

    acc_s[...] = jnp.zeros_like(acc_s)
    l_s[...] = jnp.zeros_like(l_s)
    m_s[...] = jnp.full(m_s.shape, MASKED, F32)

    def attend_chunk(kc, carry):
        bias = bias_s[kc]
        for h in range(N_HEADS_A):
            pair = slice((h // 2) * LANES, (h // 2 + 1) * LANES)
            hx = slice(h * LANES, (h + 1) * LANES)
            s = _dot_nt(qx_ref[:, hx], kb_ref[rows(kc), pair]) + bias
            m_old = m_s[h]
            m_new = jnp.maximum(m_old, jnp.max(s, axis=1, keepdims=True))
            alpha = jnp.exp(m_old - m_new)
            e = jnp.exp(s - jnp.tile(m_new, (1, n_sub)))
            l_s[h] = alpha * l_s[h] + jnp.sum(e, axis=1, keepdims=True)
            acc_s[:, hx] = alpha * acc_s[:, hx] + _dot(e.astype(BF16), vx_ref[rows(kc), hx])
            m_s[h] = m_new
        return carry

    lax.fori_loop(0, n_chunks, attend_chunk, 0)
    for hp in range(N_HEADS_A // 2):
        out = None
        for h in (2 * hp, 2 * hp + 1):
            t = acc_s[:, h * LANES:(h + 1) * LANES] * (1.0 / l_s[h])
            out = t if out is None else out + t
        a_ref[:, hp * LANES:(hp + 1) * LANES] = out.astype(BF16)

    _mem_attend(qm_ref, mk_ref[...], mv_ref[...], m_ref)


def _prompt_attention(z, mkb, mvb, bsz, seq, tq=256):
    nq = seq // tq
    k_sel = min(TOPK_MAX, seq // 4)
    qrow = lambda w: pl.BlockSpec((tq, w), lambda b, i: (b * nq + i, 0))
    brow = lambda rows, w: pl.BlockSpec((rows, w), lambda b, i: (b, 0))
    n = bsz * seq
    return pl.pallas_call(
        functools.partial(_pattn_kernel, tq=tq, k_sel=k_sel),
        grid=(bsz, nq),
        in_specs=[qrow(2 * WIDTH_A), qrow(2 * WIDTH_A), qrow(LANES), qrow(WIDTH_M),
                  brow(seq, WIDTH_A), brow(seq, 2 * WIDTH_A), brow(seq, LANES),
                  brow(N_MEM, WIDTH_M), brow(N_MEM, WIDTH_M)],
        out_specs=[qrow(WIDTH_A), qrow(WIDTH_M)],
        out_shape=[jax.ShapeDtypeStruct((n, WIDTH_A), BF16), jax.ShapeDtypeStruct((n, WIDTH_M), BF16)],
        scratch_shapes=[pltpu.VMEM((nq, tq, tq), jnp.int32), pltpu.VMEM((nq, tq, tq), F32),
                        pltpu.VMEM((tq, N_HEADS_A * LANES), F32),
                        pltpu.VMEM((N_HEADS_A, tq, LANES), F32), pltpu.VMEM((N_HEADS_A, tq, LANES), F32)],
        compiler_params=_cparams(("parallel", "arbitrary")),
        name="prompt_attn",
    )(z["qx"], z["qix"], z["wi"], z["qm"], z["kb"], z["vx"], z["ki2"], mkb, mvb)


def _sidx_kernel(pt_ref, qb_ref, wi_ref, kin_ref, *rest, n_pages, t_new):
    page_refs, out_ref = rest[:n_pages], rest[n_pages]
    qb = qb_ref[...]
    wcol = jnp.concatenate([wi_ref[:, h:h + 1] for h in range(N_IDX_HEADS)], axis=0)

    def head_sum(s):
        r = jnp.maximum(s, 0.0) * wcol
        sc = r[0:t_new]
        for h in range(1, N_IDX_HEADS):
            sc = sc + r[h * t_new:(h + 1) * t_new]
        return sc

    for j in range(n_pages):
        out_ref[:, j * PAGE_SIZE:(j + 1) * PAGE_SIZE] = head_sum(_dot(qb, page_refs[j][...].astype(BF16)))
    knew = jnp.concatenate([kin_ref[...], jnp.zeros((PAGE_SIZE - t_new, IDX_DIM), F32)], axis=0)
    sc = head_sum(_dot_nt(qb, knew.astype(BF16)))
    tpos = lax.broadcasted_iota(jnp.int32, (t_new, PAGE_SIZE), 0)
    cpos = lax.broadcasted_iota(jnp.int32, (t_new, PAGE_SIZE), 1)
    out_ref[:, n_pages * PAGE_SIZE:] = jnp.where(cpos <= tpos, sc, NEG_INF)


def _select_kernel(score_ref, bias_ref, *, k_sel):
    _select_bias(score_ref[...], k_sel, bias_ref)


def _sattn_kernel(pt_ref, q_ref, bias_ref, kn_ref, vn_ref, qmb_ref, mk_ref, mv_ref, *rest,
                  n_pages, t_new):
    k_refs, v_refs = rest[:n_pages], rest[n_pages:2 * n_pages]
    a_ref, m_ref = rest[2 * n_pages:]

    rows = N_HEADS_A * t_new
    r_h = lax.broadcasted_iota(jnp.int32, (rows, WIDTH_A), 0) // t_new
    l_h = lax.broadcasted_iota(jnp.int32, (rows, WIDTH_A), 1) // HEAD_DIM_A
    keep = r_h == l_h
    qblk = jnp.where(keep, jnp.tile(q_ref[...], (N_HEADS_A, 1)), 0.0).astype(BF16)
    bias = jnp.tile(bias_ref[...], (N_HEADS_A, 1))
    pad = jnp.zeros((PAGE_SIZE - t_new, WIDTH_A), F32)
    knew = jnp.concatenate([kn_ref[...], pad], axis=0).astype(BF16)
    vnew = jnp.concatenate([vn_ref[...], pad], axis=0).astype(BF16)
    chunks = [_dot(qblk, k_refs[j][...].astype(BF16)) for j in range(n_pages)]
    chunks.append(_dot_nt(qblk, knew))
    s = jnp.concatenate(chunks, axis=1) + bias
    e = jnp.exp(s - jnp.max(s, axis=1, keepdims=True))
    linv = 1.0 / jnp.sum(e, axis=1, keepdims=True)
    eb = e.astype(BF16)
    o = _dot(eb[:, n_pages * PAGE_SIZE:], vnew)
    for j in range(n_pages):
        o = o + _dot_nt(eb[:, j * PAGE_SIZE:(j + 1) * PAGE_SIZE], v_refs[j][...].astype(BF16))
    o = jnp.where(keep, o * linv, 0.0)
    out = o[0:t_new]
    for h in range(1, N_HEADS_A):
        out = out + o[h * t_new:(h + 1) * t_new]
    a_ref[...] = out

    rows_m = N_HEADS_M * t_new
    n_col = N_MEM * N_HEADS_M
    row_h = lax.broadcasted_iota(jnp.int32, (rows_m, n_col), 0) // t_new
    col_h = lax.broadcasted_iota(jnp.int32, (rows_m, n_col), 1) % N_HEADS_M
    sm = _dot_nt(qmb_ref[...], mk_ref[...].astype(BF16)) * HEAD_DIM_M ** -0.5
    sm = jnp.where(row_h == col_h, sm, NEG_INF)
    em = jnp.exp(sm - jnp.max(sm, axis=1, keepdims=True))
    om = _dot(em.astype(BF16), mv_ref[...].astype(BF16)) * (1.0 / jnp.sum(em, axis=1, keepdims=True))
    m_ref[...] = jnp.concatenate([om[h * t_new:(h + 1) * t_new] for h in range(N_HEADS_M)], axis=1)


def _sample_attention(z, layer, page_table, cache_kt, cache_vt, cache_kidxt, cache_mem_k2, cache_mem_v2,
                      db, t_new):
    n_pages = page_table.shape[1]
    past = n_pages * PAGE_SIZE
    s_pad = past + PAGE_SIZE
    k_sel = min(TOPK_MAX, (past + t_new) // 4)
    n = db * t_new

    def head_major(a, n_heads, head_dim):
        return a.reshape(db, t_new, n_heads, head_dim).transpose(0, 2, 1, 3).reshape(
            db, n_heads * t_new, head_dim)

    brow = lambda w: pl.BlockSpec((t_new, w), lambda b, pt: (b, 0))

    def page_spec(rows, j):
        return pl.BlockSpec((None, None, rows, PAGE_SIZE), lambda b, pt: (layer, pt[b, j], 0, 0))

    score = pl.pallas_call(
        functools.partial(_sidx_kernel, n_pages=n_pages, t_new=t_new),
        grid_spec=pltpu.PrefetchScalarGridSpec(
            num_scalar_prefetch=1, grid=(db,),
            in_specs=[pl.BlockSpec((None, N_IDX_HEADS * t_new, IDX_DIM), lambda b, pt: (b, 0, 0)),
                      brow(LANES), brow(IDX_DIM)] + [page_spec(IDX_DIM, j) for j in range(n_pages)],
            out_specs=brow(s_pad)),
        out_shape=jax.ShapeDtypeStruct((n, s_pad), F32),
        compiler_params=_cparams(("parallel",)),
        name="sample_index",
    )(page_table, head_major(z["qi"], N_IDX_HEADS, IDX_DIM), z["wi"], z["ki"], *([cache_kidxt] * n_pages))

    tr = min(256, n)
    bias = pl.pallas_call(
        functools.partial(_select_kernel, k_sel=k_sel),
        grid=(n // tr,),
        in_specs=[pl.BlockSpec((tr, s_pad), lambda i: (i, 0))],
        out_specs=pl.BlockSpec((tr, s_pad), lambda i: (i, 0)),
        out_shape=jax.ShapeDtypeStruct((n, s_pad), F32),
        compiler_params=_cparams(("parallel",)),
        name="sample_select",
    )(score)

    mem_spec = pl.BlockSpec((None, None, N_MEM * N_HEADS_M, HEAD_DIM_M), lambda b, pt: (layer, b, 0, 0))
    return pl.pallas_call(
        functools.partial(_sattn_kernel, n_pages=n_pages, t_new=t_new),
        grid_spec=pltpu.PrefetchScalarGridSpec(
            num_scalar_prefetch=1, grid=(db,),
            in_specs=[brow(WIDTH_A), brow(s_pad), brow(WIDTH_A), brow(WIDTH_A),
                      pl.BlockSpec((None, N_HEADS_M * t_new, HEAD_DIM_M), lambda b, pt: (b, 0, 0)),
                      mem_spec, mem_spec]
                     + [page_spec(WIDTH_A, j) for j in range(n_pages)] * 2,
            out_specs=[brow(WIDTH_A), brow(WIDTH_M)]),
        out_shape=[jax.ShapeDtypeStruct((n, WIDTH_A), F32), jax.ShapeDtypeStruct((n, WIDTH_M), F32)],
        compiler_params=_cparams(("parallel",)),
        name="sample_attn",
    )(page_table, z["q"].astype(F32), bias, z["k"], z["v"], head_major(z["qm"], N_HEADS_M, HEAD_DIM_M),
      cache_mem_k2, cache_mem_v2, *([cache_kt] * n_pages), *([cache_vt] * n_pages))


def _lru_gates(xc, wa_ref, ba_ref, wx_ref, bx_ref, lam_ref):
    xb = xc.astype(BF16)
    r = jax.nn.sigmoid(_dot(xb, wa_ref[...]) + ba_ref[...])
    ig = jax.nn.sigmoid(_dot(xb, wx_ref[...]) + bx_ref[...])
    log_a = -LRU_C * r * _softplus(-lam_ref[...])
    a = jnp.exp(log_a)
    u = jnp.sqrt(-jnp.tanh(log_a) * (a * a + 1.0)) * ig * xc
    return a, u


def _rnn_prompt_kernel(xr_ref, gr_ref, cw_ref, cb_ref, wa_ref, ba_ref, wx_ref, bx_ref, lam_ref,
                       r_ref, h_ref, buf_ref, prev_s, hc_s, *, tt):
    @pl.when(pl.program_id(1) == 0)
    def _():
        prev_s[...] = jnp.zeros_like(prev_s)
        hc_s[...] = jnp.zeros_like(hc_s)

    x = xr_ref[...]
    ext = jnp.concatenate([prev_s[...], x], axis=0)
    conv = cb_ref[...] + cw_ref[RNN_CONV - 1:RNN_CONV] * ext
    for j in range(1, RNN_CONV):
        conv = conv + cw_ref[RNN_CONV - 1 - j:RNN_CONV - j] * pltpu.roll(ext, j, 0)
    xc = conv[SUBLANES:]
    a, u = _lru_gates(xc, wa_ref, ba_ref, wx_ref, bx_ref, lam_ref)

    row = lax.broadcasted_iota(jnp.int32, (tt, 1), 0)
    step = 1
    while step < tt:
        m = row >= step
        u = jnp.where(m, a * pltpu.roll(u, step, 0) + u, u)
        a = jnp.where(m, a * pltpu.roll(a, step, 0), a)
        step *= 2
    h = u + a * hc_s[0:1]
    r_ref[...] = (h * _gelu(gr_ref[...])).astype(BF16)
    h_tail = h[tt - SUBLANES:]
    h_ref[...] = h_tail
    hc_s[...] = jnp.broadcast_to(h_tail[SUBLANES - 1:SUBLANES], hc_s.shape)
    buf_ref[...] = x[tt - SUBLANES:]
    prev_s[...] = x[tt - SUBLANES:]


def _rnn_prompt(xr, gr, p, bsz, seq, tt=256):
    nt = seq // tt
    row = pl.BlockSpec((tt, WIDTH_RNN), lambda b, i: (b * nt + i, 0))
    tail = pl.BlockSpec((None, SUBLANES, WIDTH_RNN), lambda b, i: (b, 0, 0))
    vec = _const_spec((1, WIDTH_RNN))
    sq = _const_spec((WIDTH_RNN, WIDTH_RNN))
    return pl.pallas_call(
        functools.partial(_rnn_prompt_kernel, tt=tt),
        grid=(bsz, nt),
        in_specs=[row, row, _const_spec((RNN_CONV, WIDTH_RNN)), vec, sq, vec, sq, vec, vec],
        out_specs=[row, tail, tail],
        out_shape=[jax.ShapeDtypeStruct((bsz * seq, WIDTH_RNN), BF16),
                   jax.ShapeDtypeStruct((bsz, SUBLANES, WIDTH_RNN), F32),
                   jax.ShapeDtypeStruct((bsz, SUBLANES, WIDTH_RNN), F32)],
        scratch_shapes=[pltpu.VMEM((SUBLANES, WIDTH_RNN), F32), pltpu.VMEM((SUBLANES, WIDTH_RNN), F32)],
        compiler_params=_cparams(("parallel", "arbitrary")),
        name="rnn_prompt",
    )(xr, gr, p["rnn_conv_w"], p["rnn_conv_b"], p["rnn_wa"], p["rnn_ba"], p["rnn_wx"], p["rnn_bx"],
      p["rnn_lambda"])


def _rnn_sample_kernel(xr_ref, gr_ref, cbuf_ref, h0_ref, cw_ref, cb_ref, wa_ref, ba_ref, wx_ref, bx_ref,
                       lam_ref, r_ref, h_ref, buf_ref, *, db, t_new):
    x = xr_ref[...]
    n = db * t_new
    ext = jnp.concatenate([cbuf_ref[...], x], axis=0)
    conv = cb_ref[...] + cw_ref[0:1] * ext[0:n]
    for j in range(1, RNN_CONV):
        conv = conv + cw_ref[j:j + 1] * ext[j * db:j * db + n]
    a, u = _lru_gates(conv, wa_ref, ba_ref, wx_ref, bx_ref, lam_ref)
    h = h0_ref[...]
    hs = []
    for t in range(t_new):
        h = a[t * db:(t + 1) * db] * h + u[t * db:(t + 1) * db]
        hs.append(h)
    r_ref[...] = jnp.concatenate(hs, axis=0) * _gelu(gr_ref[...])
    h_ref[...] = h
    buf_ref[...] = ext[n:]


def _rnn_sample(xr_tm, gr_tm, cbuf_tm, h0, p, db, t_new):
    n = db * t_new
    nb = (RNN_CONV - 1) * db
    full = lambda r, c: _const_spec((r, c))
    vec = full(1, WIDTH_RNN)
    sq = full(WIDTH_RNN, WIDTH_RNN)
    return pl.pallas_call(
        functools.partial(_rnn_sample_kernel, db=db, t_new=t_new),
        grid=(1,),
        in_specs=[full(n, WIDTH_RNN), full(n, WIDTH_RNN), full(nb, WIDTH_RNN), full(db, WIDTH_RNN),
                  full(RNN_CONV, WIDTH_RNN), vec, sq, vec, sq, vec, vec],
        out_specs=[full(n, WIDTH_RNN), full(db, WIDTH_RNN), full(nb, WIDTH_RNN)],
        out_shape=[jax.ShapeDtypeStruct((n, WIDTH_RNN), F32), jax.ShapeDtypeStruct((db, WIDTH_RNN), F32),
                   jax.ShapeDtypeStruct((nb, WIDTH_RNN), F32)],
        compiler_params=_cparams(("arbitrary",)),
        name="rnn_sample",
    )(xr_tm, gr_tm, cbuf_tm, h0, p["rnn_conv_w"], p["rnn_conv_b"], p["rnn_wa"], p["rnn_ba"],
      p["rnn_wx"], p["rnn_bx"], p["rnn_lambda"])


def _merge_kernel(x_ref, a_ref, r_ref, m_ref, gate_ref, wb_ref, wo_ref, out_ref):
    y = None
    for n, br in enumerate((a_ref, r_ref, m_ref)):
        proj = _dot(br[...].astype(BF16), wb_ref[n])
        t = gate_ref[:, n * D_MODEL:(n + 1) * D_MODEL].astype(F32) * proj
        y = t if y is None else y + t
    out_ref[...] = x_ref[...] + _dot(y.astype(BF16), wo_ref[...])


def _merge(x, a, r, m, gate, p, tm):
    n = x.shape[0]
    row = lambda w: pl.BlockSpec((tm, w), lambda i: (i, 0))
    return pl.pallas_call(
        _merge_kernel,
        grid=(n // tm,),
        in_specs=[row(D_MODEL), row(WIDTH_A), row(WIDTH_RNN), row(WIDTH_M), row(N_BRANCH * D_MODEL),
                  _const_spec((N_BRANCH, WIDTH_A, D_MODEL)), _const_spec((D_MODEL, D_MODEL))],
        out_specs=row(D_MODEL),
        out_shape=jax.ShapeDtypeStruct((n, D_MODEL), F32),
        compiler_params=_cparams(("parallel",)),
        name="merge",
    )(x, a, r, m, gate, p["w_branch"], p["w_out"])


def _ffn_kernel(*refs, tm, stride, n_prev, use_state):
    if use_state:
        x_ref, g_ref, wg_ref, wv_ref, wd_ref, cw_ref, cb_ref, st_ref, out_ref, fb_ref, xn_s = refs
    else:
        x_ref, g_ref, wg_ref, wv_ref, wd_ref, cw_ref, cb_ref, out_ref, fb_ref, xn_s, carry_s = refs
    i = pl.program_id(1)
    j = pl.program_id(2)

    @pl.when(j == 0)
    def _():
        x = x_ref[...]
        xn_s[...] = _rmsnorm_rows(x, g_ref[...]).astype(BF16)
        out_ref[...] = x

    xn = xn_s[...]
    gt = _dot(xn, wg_ref[...])
    val = _dot(xn, wv_ref[...])
    if use_state:
        prev = st_ref[...]
    else:
        @pl.when(i == 0)
        def _():
            carry_s[j] = jnp.zeros((SUBLANES, gt.shape[1]), F32)

        prev = carry_s[j]
    ext = jnp.concatenate([prev, gt], axis=0)
    conv = cb_ref[...] + cw_ref[FFN_CONV - 1:FFN_CONV] * ext
    for s in range(1, FFN_CONV):
        conv = conv + cw_ref[FFN_CONV - 1 - s:FFN_CONV - s] * pltpu.roll(ext, s * stride, 0)
    act = _gelu(conv[n_prev:]) * val
    out_ref[...] += _dot(act.astype(BF16), wd_ref[...])
    fb_ref[...] = gt[tm - n_prev:]
    if not use_state:
        carry_s[j] = gt[tm - n_prev:]


def _ffn(x, p, *, groups, rows_per_group, tm, stride, state=None, tf=512):
    nt = rows_per_group // tm
    nj = D_FF // tf
    use_state = state is not None
    n_prev = (FFN_CONV - 1) * stride if use_state else SUBLANES
    n = groups * rows_per_group
    xrow = pl.BlockSpec((tm, D_MODEL), lambda b, i, j: (b * nt + i, 0))
    in_specs = [xrow, _const_spec((1, D_MODEL)),
                pl.BlockSpec((D_MODEL, tf), lambda b, i, j: (0, j)),
                pl.BlockSpec((D_MODEL, tf), lambda b, i, j: (0, nj + j)),
                pl.BlockSpec((tf, D_MODEL), lambda b, i, j: (j, 0)),
                pl.BlockSpec((FFN_CONV, tf), lambda b, i, j: (0, j)),
                pl.BlockSpec((1, tf), lambda b, i, j: (0, j))]
    args = [x, p["norm_ffn_g"], p["w_ffn_up"], p["w_ffn_up"], p["w_ffn_down"], p["ffn_conv_w"],
            p["ffn_conv_b"]]
    scratch = [pltpu.VMEM((tm, D_MODEL), BF16)]
    if use_state:
        in_specs.append(pl.BlockSpec((n_prev, tf), lambda b, i, j: (0, j)))
        args.append(state)
    else:
        scratch.append(pltpu.VMEM((nj, SUBLANES, tf), F32))
    return pl.pallas_call(
        functools.partial(_ffn_kernel, tm=tm, stride=stride, n_prev=n_prev, use_state=use_state),
        grid=(groups, nt, nj),
        in_specs=in_specs,
        out_specs=[xrow, pl.BlockSpec((None, n_prev, tf), lambda b, i, j: (b * nt + i, 0, j))],
        out_shape=[jax.ShapeDtypeStruct((n, D_MODEL), F32),
                   jax.ShapeDtypeStruct((groups * nt, n_prev, D_FF), F32)],
        scratch_shapes=scratch,
        compiler_params=_cparams(("parallel", "arbitrary", "arbitrary")),
        name="ffn",
    )(*args)


def _rope_tables(pos):
    half = ROPE_DIM // 2
    freqs = jnp.power(ROPE_THETA, -jnp.arange(half, dtype=F32) * 2.0 / ROPE_DIM)
    ang = pos.astype(F32)[:, None] * freqs[None, :]
    d = jnp.arange(LANES) % HEAD_DIM_A
    cos = jnp.cos(ang)[:, d % half]
    sin = jnp.sin(ang)[:, d % half]
    c = jnp.where(d < ROPE_DIM, cos, 1.0)
    s1 = jnp.where(d < half, -sin, 0.0)
    s2 = jnp.where((d >= half) & (d < ROPE_DIM), sin, 0.0)
    return c, s1, s2


def _block_diag(w):
    nl, nb, c, _ = w.shape
    eye = jnp.eye(nb, dtype=w.dtype)
    return (w[:, :, :, None, :] * eye[None, :, None, :, None]).reshape(nl, nb * c, nb * c)


def _prep_params(norm_mix_g, w_in, q_norm_g, k_norm_g, mq_norm_g, mk_norm_g, mem_norm_g, w_mem_kv,
                 rnn_conv_w, rnn_conv_b, rnn_wa, rnn_ba, rnn_wx, rnn_bx, rnn_lambda,
                 w_branch, w_out, norm_ffn_g, w_ffn_up, ffn_conv_w, ffn_conv_b, w_ffn_down):
    q, k, v, qi, ki, wi, xr, gr, qm, gates = jnp.split(w_in, IN_SPLITS, axis=-1)
    wi_p = jnp.pad(wi, ((0, 0), (0, 0), (0, LANES - N_IDX_HEADS)))
    w_in_p = jnp.concatenate([q, k, v, qi, ki, ki, wi_p, xr, gr, qm, gates], axis=-1).astype(BF16)
    row = lambda a: a[:, None, :]
    stacked = {
        "norm_mix_g": row(norm_mix_g), "w_in": w_in_p,
        "q_norm_g": row(jnp.tile(q_norm_g, (1, N_HEADS_A))), "k_norm_g": row(jnp.tile(k_norm_g, (1, N_HEADS_A))),
        "mq_norm_g": row(jnp.tile(mq_norm_g, (1, N_HEADS_M))), "mk_norm_g": row(jnp.tile(mk_norm_g, (1, N_HEADS_M))),
        "mem_norm_g": row(mem_norm_g), "w_mem_kv": w_mem_kv.astype(BF16),
        "rnn_conv_w": rnn_conv_w, "rnn_conv_b": row(rnn_conv_b),
        "rnn_wa": _block_diag(rnn_wa).astype(BF16), "rnn_ba": row(rnn_ba),
        "rnn_wx": _block_diag(rnn_wx).astype(BF16), "rnn_bx": row(rnn_bx),
        "rnn_lambda": row(rnn_lambda),
        "w_branch": w_branch.astype(BF16), "w_out": w_out.astype(BF16),
        "norm_ffn_g": row(norm_ffn_g), "w_ffn_up": w_ffn_up.astype(BF16),
        "ffn_conv_w": ffn_conv_w, "ffn_conv_b": row(ffn_conv_b), "w_ffn_down": w_ffn_down.astype(BF16),
    }
    ones = lambda n, c: jnp.kron(jnp.eye(n, dtype=F32), jnp.ones((c, c), F32)).astype(BF16)
    shared = {"bd64": ones(N_HEADS_A, HEAD_DIM_A), "bd128": ones(N_HEADS_M, HEAD_DIM_M)}
    return [dict({k_: v_[l] for k_, v_ in stacked.items()}, **shared) for l in range(DEPTH)]


def kernel(x_prompt, x_sample, mem_prompt, cache_k, cache_v, cache_kidx, page_table, cache_mem_k, cache_mem_v, state_rnn_h, state_rnn_conv, state_ffn_conv, norm_mix_g, w_in, q_norm_g, k_norm_g, mq_norm_g, mk_norm_g, mem_norm_g, w_mem_kv, rnn_conv_w, rnn_conv_b, rnn_wa, rnn_ba, rnn_wx, rnn_bx, rnn_lambda, w_branch, w_out, norm_ffn_g, w_ffn_up, ffn_conv_w, ffn_conv_b, w_ffn_down):
    bsz, seq, _ = x_prompt.shape
    db, t_new, _ = x_sample.shape
    n_pool = cache_k.shape[1]
    past = page_table.shape[1] * PAGE_SIZE
    params = _prep_params(norm_mix_g, w_in, q_norm_g, k_norm_g, mq_norm_g, mk_norm_g, mem_norm_g, w_mem_kv,
                          rnn_conv_w, rnn_conv_b, rnn_wa, rnn_ba, rnn_wx, rnn_bx, rnn_lambda,
                          w_branch, w_out, norm_ffn_g, w_ffn_up, ffn_conv_w, ffn_conv_b, w_ffn_down)
    tabs_p = _rope_tables(jnp.tile(jnp.arange(seq), bsz))
    tabs_s = _rope_tables(jnp.tile(past + jnp.arange(t_new), db))
    cache_kt = cache_k.transpose(0, 1, 3, 4, 2).reshape(DEPTH, n_pool, WIDTH_A, PAGE_SIZE)
    cache_vt = cache_v.transpose(0, 1, 3, 4, 2).reshape(DEPTH, n_pool, WIDTH_A, PAGE_SIZE)
    cache_kidxt = cache_kidx.transpose(0, 1, 3, 2)
    cache_mem_k2 = cache_mem_k.reshape(DEPTH, db, N_MEM * N_HEADS_M, HEAD_DIM_M)
    cache_mem_v2 = cache_mem_v.reshape(DEPTH, db, N_MEM * N_HEADS_M, HEAD_DIM_M)
    mem = mem_prompt.reshape(bsz * N_MEM, D_MODEL)

    def to_tm(a):
        return a.reshape(db, t_new, -1).transpose(1, 0, 2).reshape(db * t_new, -1)

    def to_bm(a):
        return a.reshape(t_new, db, -1).transpose(1, 0, 2).reshape(db * t_new, -1)

    xp = x_prompt.reshape(bsz * seq, D_MODEL)
    xs = x_sample.reshape(db * t_new, D_MODEL)
    outs = [[] for _ in range(14)]
    tm_p = min(512, seq)
    tm_s = min(512, db * t_new)
    for l in range(DEPTH):
        p = params[l]
        mk, mkb, mv, mvb = _memkv(mem, p)
        zp = _inproj(xp, p, tabs_p, tm=tm_p)
        a_p, m_p = _prompt_attention(zp, mkb, mvb, bsz, seq)
        r_p, h_p, rb_p = _rnn_prompt(zp["xr"], zp["gr"], p, bsz, seq)
        xp = _merge(xp, a_p, r_p, m_p, zp["gate"], p, tm=tm_p)
        xp, fb_p = _ffn(xp, p, groups=bsz, rows_per_group=seq, tm=tm_p, stride=1)
        zs = _inproj(xs, p, tabs_s, tm=tm_s)
        a_s, m_s = _sample_attention(zs, l, page_table, cache_kt, cache_vt, cache_kidxt,
                                     cache_mem_k2, cache_mem_v2, db, t_new)
        cbuf_tm = state_rnn_conv[l].transpose(1, 0, 2).reshape((RNN_CONV - 1) * db, WIDTH_RNN)
        r_s_tm, h_s, rb_s_tm = _rnn_sample(to_tm(zs["xr"]), to_tm(zs["gr"]), cbuf_tm, state_rnn_h[l],
                                            p, db, t_new)
        xs = _merge(xs, a_s, to_bm(r_s_tm), m_s, zs["gate"], p, tm=tm_s)
        fst_tm = state_ffn_conv[l].transpose(1, 0, 2).reshape((FFN_CONV - 1) * db, D_FF)
        xs_tm, fb_s_tm = _ffn(to_tm(xs), p, groups=1, rows_per_group=db * t_new, tm=db * t_new,
                              stride=db, state=fst_tm)
        xs = to_bm(xs_tm)

        new = (zp["k"].reshape(bsz, seq, N_HEADS_A, HEAD_DIM_A), zs["k"].reshape(db, t_new, N_HEADS_A, HEAD_DIM_A),
               zp["v"].reshape(bsz, seq, N_HEADS_A, HEAD_DIM_A), zs["v"].reshape(db, t_new, N_HEADS_A, HEAD_DIM_A),
               zp["ki"].reshape(bsz, seq, IDX_DIM), zs["ki"].reshape(db, t_new, IDX_DIM),
               mk.reshape(bsz, N_MEM, N_HEADS_M, HEAD_DIM_M), mv.reshape(bsz, N_MEM, N_HEADS_M, HEAD_DIM_M),
               h_p[:, SUBLANES - 1], h_s,
               rb_p[:, SUBLANES - (RNN_CONV - 1):],
               rb_s_tm.reshape(RNN_CONV - 1, db, WIDTH_RNN).transpose(1, 0, 2),
               fb_p.reshape(bsz, -1, SUBLANES, D_FF)[:, -1, SUBLANES - (FFN_CONV - 1):],
               fb_s_tm.reshape(FFN_CONV - 1, db, D_FF).transpose(1, 0, 2))
        for lst, val in zip(outs, new):
            lst.append(val)
    return (xp.reshape(bsz, seq, D_MODEL), xs.reshape(db, t_new, D_MODEL)) + tuple(jnp.stack(o) for o in outs)
```

```python
import functools

import jax
import jax.numpy as jnp
from jax import lax
from jax.experimental import pallas as pl
from jax.experimental.pallas import tpu as pltpu

F32 = jnp.float32
BF16 = jnp.bfloat16

D_MODEL = 1024
DEPTH = 4
PAGE_SIZE = 128
N_HEADS_A = 8
HEAD_DIM_A = 64
WIDTH_A = 512
N_IDX_HEADS = 8
IDX_DIM = 64
TOPK_MAX = 256
INDEX_WEIGHT_SCALE = N_IDX_HEADS ** -0.5 * IDX_DIM ** -0.5
WIDTH_RNN = 512
N_RNN_BLOCKS = 8
RNN_CONV = 4
LRU_C = 8.0
N_MEM = 256
N_HEADS_M = 4
HEAD_DIM_M = 128
WIDTH_M = 512
N_BRANCH = 3
D_FF = 3 * D_MODEL
FFN_CONV = 3
ROPE_THETA = 500000.0
ROPE_DIM = 16
EPS = 1e-6

IN_SIZES = (WIDTH_A, WIDTH_A, WIDTH_A, N_IDX_HEADS * IDX_DIM, IDX_DIM, N_IDX_HEADS,
            WIDTH_RNN, WIDTH_RNN, WIDTH_M, N_BRANCH * D_MODEL)
IN_SPLITS = tuple(sum(IN_SIZES[:i + 1]) for i in range(len(IN_SIZES) - 1))

LANES = 128
SUBLANES = 8
OFF_Q, OFF_K, OFF_V, OFF_QI = 0, 512, 1024, 1536
OFF_KI2 = 2048
OFF_WI = OFF_KI2 + LANES
OFF_XR = OFF_WI + LANES
OFF_GR = OFF_XR + WIDTH_RNN
OFF_QM = OFF_GR + WIDTH_RNN
OFF_G = OFF_QM + WIDTH_M
D_IN_P = OFF_G + N_BRANCH * D_MODEL

VMEM_LIMIT = 56 * 1024 * 1024

NEG_INF = float("-inf")
INT_MIN = -2 ** 31
NEG_INF_KEY = INT_MIN + 0x7FFFFF
MASKED = -1e30


def _cparams(sem):
    return pltpu.CompilerParams(dimension_semantics=sem, vmem_limit_bytes=VMEM_LIMIT)


def _const_spec(shape):
    nd = len(shape)
    return pl.BlockSpec(shape, lambda *_: (0,) * nd)


def _gelu(x):
    return 0.5 * x * (1.0 + jnp.tanh(0.7978845608028654 * (x + 0.044715 * (x * x * x))))


def _softplus(z):
    return jnp.maximum(z, 0.0) + jnp.log1p(jnp.exp(-jnp.abs(z)))


def _rmsnorm_rows(x, g):
    return x * lax.rsqrt(jnp.mean(x * x, axis=-1, keepdims=True) + EPS) * g


def _dot(a, b):
    return jnp.dot(a, b, preferred_element_type=F32)


def _dot_nt(a, b):
    return lax.dot_general(a, b, (((1,), (1,)), ((), ())), preferred_element_type=F32)


def _order_key(score):
    score = jnp.where(score == 0.0, 0.0, score)
    bits = lax.bitcast_convert_type(score, jnp.int32)
    return jnp.where(bits < 0, bits ^ jnp.int32(0x7FFFFFFF), bits)


def _select_bias(score, k_sel, bias_ref):
    rows, s_len = score.shape
    valid = score > NEG_INF
    score = jnp.where(score == 0.0, 0.0, score)
    bits = lax.bitcast_convert_type(score, jnp.int32)
    key = jnp.where(bits < 0, bits ^ jnp.int32(0x7FFFFFFF), bits)
    kf = jnp.float32(k_sel)

    def count_ge(c):
        return jnp.sum(jnp.where(key >= c, 1.0, 0.0), axis=1, keepdims=True)

    thr = jnp.where(count_ge(jnp.int32(0)) >= kf, jnp.int32(0), jnp.int32(INT_MIN))

    def body(it, thr):
        cand = thr + jnp.left_shift(jnp.int32(1), 30 - it)
        return jnp.where(count_ge(cand) >= kf, cand, thr)

    thr = lax.fori_loop(0, 31, body, thr)
    n_ge = count_ge(thr)
    bias_ref[...] = jnp.where((key >= thr) & valid, 0.0, NEG_INF)

    tie = jnp.where((n_ge > kf) & (thr > NEG_INF_KEY), 1.0, 0.0)

    @pl.when(jnp.max(tie) > 0.0)
    def _():
        gt = jnp.where((key > thr) & valid, 1.0, 0.0)
        eq = jnp.where((key == thr) & valid, 1.0, 0.0)
        need = kf - jnp.sum(gt, axis=1, keepdims=True)
        r_i = lax.broadcasted_iota(jnp.int32, (LANES, LANES), 0)
        c_i = lax.broadcasted_iota(jnp.int32, (LANES, LANES), 1)
        tri = jnp.where(r_i < c_i, 1.0, 0.0).astype(BF16)
        run = jnp.zeros((rows, 1), F32)
        for c in range(s_len // LANES):
            sl = slice(c * LANES, (c + 1) * LANES)
            eq_c = eq[:, sl]
            before = _dot(eq_c.astype(BF16), tri) + run
            sel = (gt[:, sl] > 0.0) | ((eq_c > 0.0) & (before < need))
            bias_ref[:, sl] = jnp.where(sel, 0.0, NEG_INF)
            run = run + jnp.sum(eq_c, axis=1, keepdims=True)


def _inproj_kernel(x_ref, g_ref, w_ref, bd64_ref, bd128_ref, qg_ref, kg_ref, mqg_ref,
                   c_ref, s1_ref, s2_ref,
                   q_ref, qx_ref, k_ref, kb_ref, v_ref, vx_ref, qi_ref, qix_ref, ki_ref, ki2_ref, wi_ref,
                   xr_ref, gr_ref, qm_ref, gate_ref):
    xn = _rmsnorm_rows(x_ref[...], g_ref[...]).astype(BF16)
    lo, hi = _half_masks()

    def expand_heads(z):
        parts = []
        for hp in range(z.shape[1] // LANES):
            pair = z[:, hp * LANES:(hp + 1) * LANES]
            parts += [jnp.where(lo, pair, 0.0), jnp.where(hi, pair, 0.0)]
        return jnp.concatenate(parts, axis=1).astype(BF16)

    def seg(off, width):
        return _dot(xn, w_ref[:, off:off + width])

    c1, s1, s2 = c_ref[...], s1_ref[...], s2_ref[...]

    def rope(z):
        w = z.shape[1]
        n = w // LANES
        cc, a, b = (jnp.tile(t, (1, n)) if n > 1 else t for t in (c1, s1, s2))
        return z * cc + pltpu.roll(z, w - ROPE_DIM // 2, 1) * a + pltpu.roll(z, ROPE_DIM // 2, 1) * b

    def headnorm(z, bd_ref, dim, g):
        ss = _dot((z * z).astype(BF16), bd_ref[...]) * (1.0 / dim)
        return z * lax.rsqrt(ss + EPS) * g

    q = rope(headnorm(seg(OFF_Q, WIDTH_A), bd64_ref, HEAD_DIM_A, qg_ref[...]))
    q = q * HEAD_DIM_A ** -0.5
    q_ref[...] = q.astype(BF16)
    qx_ref[...] = expand_heads(q)
    k = rope(headnorm(seg(OFF_K, WIDTH_A), bd64_ref, HEAD_DIM_A, kg_ref[...]))
    k_ref[...] = k
    kb_ref[...] = k.astype(BF16)
    v = seg(OFF_V, WIDTH_A)
    v_ref[...] = v
    vx_ref[...] = expand_heads(v)
    qi = rope(seg(OFF_QI, WIDTH_A))
    qi_ref[...] = qi.astype(BF16)
    qix_ref[...] = expand_heads(qi)
    ki2 = rope(seg(OFF_KI2, LANES))
    ki2_ref[...] = ki2.astype(BF16)
    ki_ref[...] = ki2[:, :IDX_DIM]
    wi_ref[...] = seg(OFF_WI, LANES) * INDEX_WEIGHT_SCALE
    xr_ref[...] = seg(OFF_XR, WIDTH_RNN)
    gr_ref[...] = seg(OFF_GR, WIDTH_RNN)
    qm_ref[...] = headnorm(seg(OFF_QM, WIDTH_M), bd128_ref, HEAD_DIM_M, mqg_ref[...]).astype(BF16)
    for c in range(N_BRANCH * D_MODEL // 512):
        gate_ref[:, c * 512:(c + 1) * 512] = jax.nn.sigmoid(seg(OFF_G + c * 512, 512)).astype(BF16)


def _inproj(x, p, rope_tabs, tm):
    n = x.shape[0]
    row = lambda w: pl.BlockSpec((tm, w), lambda i: (i, 0))
    outs = [("q", WIDTH_A, BF16), ("qx", 2 * WIDTH_A, BF16), ("k", WIDTH_A, F32), ("kb", WIDTH_A, BF16),
            ("v", WIDTH_A, F32), ("vx", 2 * WIDTH_A, BF16), ("qi", WIDTH_A, BF16),
            ("qix", 2 * WIDTH_A, BF16), ("ki", IDX_DIM, F32), ("ki2", LANES, BF16),
            ("wi", LANES, F32), ("xr", WIDTH_RNN, F32), ("gr", WIDTH_RNN, F32), ("qm", WIDTH_M, BF16),
            ("gate", N_BRANCH * D_MODEL, BF16)]
    res = pl.pallas_call(
        _inproj_kernel,
        grid=(n // tm,),
        in_specs=[row(D_MODEL), _const_spec((1, D_MODEL)),
                  pl.BlockSpec((D_MODEL, D_IN_P), lambda i: (0, 0), pipeline_mode=pl.Buffered(1)),
                  _const_spec((WIDTH_A, WIDTH_A)), _const_spec((WIDTH_M, WIDTH_M)),
                  _const_spec((1, WIDTH_A)), _const_spec((1, WIDTH_A)), _const_spec((1, WIDTH_M)),
                  row(LANES), row(LANES), row(LANES)],
        out_specs=[row(w) for _, w, _ in outs],
        out_shape=[jax.ShapeDtypeStruct((n, w), dt) for _, w, dt in outs],
        compiler_params=_cparams(("parallel",)),
        name="inproj",
    )(x, p["norm_mix_g"], p["w_in"], p["bd64"], p["bd128"], p["q_norm_g"], p["k_norm_g"],
      p["mq_norm_g"], *rope_tabs)
    return {name: r for (name, _, _), r in zip(outs, res)}


def _memkv_kernel(m_ref, g_ref, w_ref, bd128_ref, kg_ref, mk_ref, mkb_ref, mv_ref, mvb_ref):
    xn = _rmsnorm_rows(m_ref[...], g_ref[...]).astype(BF16)
    zk = _dot(xn, w_ref[:, :WIDTH_M])
    ss = _dot((zk * zk).astype(BF16), bd128_ref[...]) * (1.0 / HEAD_DIM_M)
    mk = zk * lax.rsqrt(ss + EPS) * kg_ref[...]
    mk_ref[...] = mk
    mkb_ref[...] = mk.astype(BF16)
    mv = _dot(xn, w_ref[:, WIDTH_M:])
    mv_ref[...] = mv
    mvb_ref[...] = mv.astype(BF16)


def _memkv(mem, p, tm=512):
    n = mem.shape[0]
    row = lambda w: pl.BlockSpec((tm, w), lambda i: (i, 0))
    return pl.pallas_call(
        _memkv_kernel,
        grid=(n // tm,),
        in_specs=[row(D_MODEL), _const_spec((1, D_MODEL)), _const_spec((D_MODEL, 2 * WIDTH_M)),
                  _const_spec((WIDTH_M, WIDTH_M)), _const_spec((1, WIDTH_M))],
        out_specs=[row(WIDTH_M)] * 4,
        out_shape=[jax.ShapeDtypeStruct((n, WIDTH_M), dt) for dt in (F32, BF16, F32, BF16)],
        compiler_params=_cparams(("parallel",)),
        name="memkv",
    )(mem, p["mem_norm_g"], p["w_mem_kv"], p["bd128"], p["mk_norm_g"])


def _half_masks():
    lane = lax.broadcasted_iota(jnp.int32, (1, LANES), 1)
    return lane < HEAD_DIM_A, lane >= HEAD_DIM_A


def _mem_attend(qm_ref, mk, mv, m_ref):
    for h in range(N_HEADS_M):
        sl = slice(h * HEAD_DIM_M, (h + 1) * HEAD_DIM_M)
        s = _dot_nt(qm_ref[:, sl], mk[:, sl]) * HEAD_DIM_M ** -0.5
        e = jnp.exp(s - jnp.max(s, axis=1, keepdims=True))
        o = _dot(e.astype(BF16), mv[:, sl])
        m_ref[:, sl] = (o * (1.0 / jnp.sum(e, axis=1, keepdims=True))).astype(m_ref.dtype)


def _pattn_kernel(qx_ref, qix_ref, wit_ref, qm_ref, kb_ref, vx_ref, ki2_ref, mk_ref, mv_ref,
                  a_ref, m_ref, key_s, bias_s, acc_s, m_s, l_s, *, tq, k_sel):
    i = pl.program_id(1)
    n_chunks = i + 1
    kf = jnp.float32(k_sel)
    n_sub = tq // LANES

    def rows(kc):
        return pl.ds(pl.multiple_of(kc * tq, tq), tq)

    qpos = i * tq + lax.broadcasted_iota(jnp.int32, (1, tq), 1)

    def index_chunk(kc, carry):
        ki2 = ki2_ref[rows(kc), :]
        score = jnp.zeros((tq, tq), F32)
        for h in range(N_IDX_HEADS):
            s = _dot_nt(ki2, qix_ref[:, h * LANES:(h + 1) * LANES])
            score = score + jnp.maximum(s, 0.0) * wit_ref[h:h + 1, :]
        kpos = kc * tq + lax.broadcasted_iota(jnp.int32, (tq, 1), 0)
        key_s[kc] = _order_key(jnp.where(kpos <= qpos, score, NEG_INF))
        return carry

    lax.fori_loop(0, n_chunks, index_chunk, 0)

    def count(pred):
        def body(kc, acc):
            hit = jnp.where(pred(key_s[kc]), 1.0, 0.0)
            return acc + jnp.sum(hit.reshape(tq // SUBLANES, SUBLANES, tq), axis=0)

        acc = lax.fori_loop(0, n_chunks, body, jnp.zeros((SUBLANES, tq), F32))
        return jnp.sum(acc, axis=0, keepdims=True)

    thr = jnp.where(count(lambda k: k >= 0) >= kf, jnp.int32(0), jnp.int32(INT_MIN))

    def bisect(it, thr):
        cand = thr + jnp.left_shift(jnp.int32(1), 30 - it)
        return jnp.where(count(lambda k: k >= cand) >= kf, cand, thr)

    thr = lax.fori_loop(0, 31, bisect, thr)
    n_ge = count(lambda k: k >= thr)

    def bias_chunk(kc, carry):
        k = key_s[kc]
        bias_s[kc] = jnp.where((k >= thr) & (k > NEG_INF_KEY), 0.0, MASKED).T
        return carry

    lax.fori_loop(0, n_chunks, bias_chunk, 0)

    tie = jnp.where((n_ge > kf) & (thr > NEG_INF_KEY), 1.0, 0.0)

    @pl.when(jnp.max(tie) > 0.0)
    def _():
        need = kf - count(lambda k: k > thr)
        r_i = lax.broadcasted_iota(jnp.int32, (LANES, LANES), 0)
        c_i = lax.broadcasted_iota(jnp.int32, (LANES, LANES), 1)
        tri = jnp.where(c_i < r_i, 1.0, 0.0).astype(BF16)

        def tie_chunk(kc, run):
            k = key_s[kc]
            for c in range(n_sub):
                sl = slice(c * LANES, (c + 1) * LANES)
                eq = jnp.where((k[sl] == thr) & (k[sl] > NEG_INF_KEY), 1.0, 0.0)
                before = _dot(tri, eq.astype(BF16)) + run
                sel = (k[sl] > thr) | ((eq > 0.0) & (before < need))
                bias_s[kc, :, sl] = jnp.where(sel, 0.0, MASKED).T
                run = run + jnp.sum(eq, axis=0, keepdims=True)
            return run

        lax.fori_loop(0, n_chunks, tie_chunk, jnp.zeros((1, tq), F32))


    acc_s[...] = jnp.zeros_like(acc_s)
    l_s[...] = jnp.zeros_like(l_s)
    m_s[...] = jnp.full(m_s.shape, MASKED, F32)

    def attend_chunk(kc, carry):
        bias = bias_s[kc]
        for h in range(N_HEADS_A):
            pair = slice((h // 2) * LANES, (h // 2 + 1) * LANES)
            hx = slice(h * LANES, (h + 1) * LANES)
            s = _dot_nt(qx_ref[:, hx], kb_ref[rows(kc), pair]) + bias
            m_old = m_s[h]
            m_new = jnp.maximum(m_old, jnp.max(s, axis=1, keepdims=True))
            alpha = jnp.exp(m_old - m_new)
            e = jnp.exp(s - jnp.tile(m_new, (1, n_sub)))
            l_s[h] = alpha * l_s[h] + jnp.sum(e, axis=1, keepdims=True)
            acc_s[:, hx] = alpha * acc_s[:, hx] + _dot(e.astype(BF16), vx_ref[rows(kc), hx])
            m_s[h] = m_new
        return carry

    lax.fori_loop(0, n_chunks, attend_chunk, 0)
    for hp in range(N_HEADS_A // 2):
        out = None
        for h in (2 * hp, 2 * hp + 1):
            t = acc_s[:, h * LANES:(h + 1) * LANES] * (1.0 / l_s[h])
            out = t if out is None else out + t
        a_ref[:, hp * LANES:(hp + 1) * LANES] = out.astype(BF16)

    _mem_attend(qm_ref, mk_ref[...], mv_ref[...], m_ref)


def _prompt_attention(z, mkb, mvb, bsz, seq, tq=256):
    nq = seq // tq
    k_sel = min(TOPK_MAX, seq // 4)
    qrow = lambda w: pl.BlockSpec((tq, w), lambda b, i: (b * nq + i, 0))
    brow = lambda rows, w: pl.BlockSpec((rows, w), lambda b, i: (b, 0))
    n = bsz * seq
    return pl.pallas_call(
        functools.partial(_pattn_kernel, tq=tq, k_sel=k_sel),
        grid=(bsz, nq),
        in_specs=[qrow(2 * WIDTH_A), qrow(2 * WIDTH_A),
                  pl.BlockSpec((N_IDX_HEADS, tq), lambda b, i: (0, b * nq + i)), qrow(WIDTH_M),
                  brow(seq, WIDTH_A), brow(seq, 2 * WIDTH_A), brow(seq, LANES),
                  brow(N_MEM, WIDTH_M), brow(N_MEM, WIDTH_M)],
        out_specs=[qrow(WIDTH_A), qrow(WIDTH_M)],
        out_shape=[jax.ShapeDtypeStruct((n, WIDTH_A), BF16), jax.ShapeDtypeStruct((n, WIDTH_M), BF16)],
        scratch_shapes=[pltpu.VMEM((nq, tq, tq), jnp.int32), pltpu.VMEM((nq, tq, tq), F32),
                        pltpu.VMEM((tq, N_HEADS_A * LANES), F32),
                        pltpu.VMEM((N_HEADS_A, tq, LANES), F32), pltpu.VMEM((N_HEADS_A, tq, LANES), F32)],
        compiler_params=_cparams(("parallel", "arbitrary")),
        name="prompt_attn",
    )(z["qx"], z["qix"], z["wi"][:, :N_IDX_HEADS].T, z["qm"], z["kb"], z["vx"], z["ki2"], mkb, mvb)


def _sidx_kernel(pt_ref, qb_ref, wi_ref, kin_ref, *rest, n_pages, t_new):
    page_refs, out_ref = rest[:n_pages], rest[n_pages]
    qb = qb_ref[...]
    wcol = jnp.concatenate([wi_ref[:, h:h + 1] for h in range(N_IDX_HEADS)], axis=0)

    def head_sum(s):
        r = jnp.maximum(s, 0.0) * wcol
        sc = r[0:t_new]
        for h in range(1, N_IDX_HEADS):
            sc = sc + r[h * t_new:(h + 1) * t_new]
        return sc

    for j in range(n_pages):
        out_ref[:, j * PAGE_SIZE:(j + 1) * PAGE_SIZE] = head_sum(_dot(qb, page_refs[j][...].astype(BF16)))
    knew = jnp.concatenate([kin_ref[...], jnp.zeros((PAGE_SIZE - t_new, IDX_DIM), F32)], axis=0)
    sc = head_sum(_dot_nt(qb, knew.astype(BF16)))
    tpos = lax.broadcasted_iota(jnp.int32, (t_new, PAGE_SIZE), 0)
    cpos = lax.broadcasted_iota(jnp.int32, (t_new, PAGE_SIZE), 1)
    out_ref[:, n_pages * PAGE_SIZE:] = jnp.where(cpos <= tpos, sc, NEG_INF)


def _select_kernel(score_ref, bias_ref, *, k_sel):
    _select_bias(score_ref[...], k_sel, bias_ref)


def _sattn_kernel(pt_ref, q_ref, bias_ref, kn_ref, vn_ref, qmb_ref, mk_ref, mv_ref, *rest,
                  n_pages, t_new):
    k_refs, v_refs = rest[:n_pages], rest[n_pages:2 * n_pages]
    a_ref, m_ref = rest[2 * n_pages:]

    rows = N_HEADS_A * t_new
    r_h = lax.broadcasted_iota(jnp.int32, (rows, WIDTH_A), 0) // t_new
    l_h = lax.broadcasted_iota(jnp.int32, (rows, WIDTH_A), 1) // HEAD_DIM_A
    keep = r_h == l_h
    qblk = jnp.where(keep, jnp.tile(q_ref[...], (N_HEADS_A, 1)), 0.0).astype(BF16)
    bias = jnp.tile(bias_ref[...], (N_HEADS_A, 1))
    pad = jnp.zeros((PAGE_SIZE - t_new, WIDTH_A), F32)
    knew = jnp.concatenate([kn_ref[...], pad], axis=0).astype(BF16)
    vnew = jnp.concatenate([vn_ref[...], pad], axis=0).astype(BF16)
    chunks = [_dot(qblk, k_refs[j][...].astype(BF16)) for j in range(n_pages)]
    chunks.append(_dot_nt(qblk, knew))
    s = jnp.concatenate(chunks, axis=1) + bias
    e = jnp.exp(s - jnp.max(s, axis=1, keepdims=True))
    linv = 1.0 / jnp.sum(e, axis=1, keepdims=True)
    eb = e.astype(BF16)
    o = _dot(eb[:, n_pages * PAGE_SIZE:], vnew)
    for j in range(n_pages):
        o = o + _dot_nt(eb[:, j * PAGE_SIZE:(j + 1) * PAGE_SIZE], v_refs[j][...].astype(BF16))
    o = jnp.where(keep, o * linv, 0.0)
    out = o[0:t_new]
    for h in range(1, N_HEADS_A):
        out = out + o[h * t_new:(h + 1) * t_new]
    a_ref[...] = out

    rows_m = N_HEADS_M * t_new
    n_col = N_MEM * N_HEADS_M
    row_h = lax.broadcasted_iota(jnp.int32, (rows_m, n_col), 0) // t_new
    col_h = lax.broadcasted_iota(jnp.int32, (rows_m, n_col), 1) % N_HEADS_M
    sm = _dot_nt(qmb_ref[...], mk_ref[...].astype(BF16)) * HEAD_DIM_M ** -0.5
    sm = jnp.where(row_h == col_h, sm, NEG_INF)
    em = jnp.exp(sm - jnp.max(sm, axis=1, keepdims=True))
    om = _dot(em.astype(BF16), mv_ref[...].astype(BF16)) * (1.0 / jnp.sum(em, axis=1, keepdims=True))
    m_ref[...] = jnp.concatenate([om[h * t_new:(h + 1) * t_new] for h in range(N_HEADS_M)], axis=1)


def _sample_attention(z, layer, page_table, cache_kt, cache_vt, cache_kidxt, cache_mem_k2, cache_mem_v2,
                      db, t_new):
    n_pages = page_table.shape[1]
    past = n_pages * PAGE_SIZE
    s_pad = past + PAGE_SIZE
    k_sel = min(TOPK_MAX, (past + t_new) // 4)
    n = db * t_new

    def head_major(a, n_heads, head_dim):
        return a.reshape(db, t_new, n_heads, head_dim).transpose(0, 2, 1, 3).reshape(
            db, n_heads * t_new, head_dim)

    brow = lambda w: pl.BlockSpec((t_new, w), lambda b, pt: (b, 0))

    def page_spec(rows, j):
        return pl.BlockSpec((None, None, rows, PAGE_SIZE), lambda b, pt: (layer, pt[b, j], 0, 0))

    score = pl.pallas_call(
        functools.partial(_sidx_kernel, n_pages=n_pages, t_new=t_new),
        grid_spec=pltpu.PrefetchScalarGridSpec(
            num_scalar_prefetch=1, grid=(db,),
            in_specs=[pl.BlockSpec((None, N_IDX_HEADS * t_new, IDX_DIM), lambda b, pt: (b, 0, 0)),
                      brow(LANES), brow(IDX_DIM)] + [page_spec(IDX_DIM, j) for j in range(n_pages)],
            out_specs=brow(s_pad)),
        out_shape=jax.ShapeDtypeStruct((n, s_pad), F32),
        compiler_params=_cparams(("parallel",)),
        name="sample_index",
    )(page_table, head_major(z["qi"], N_IDX_HEADS, IDX_DIM), z["wi"], z["ki"], *([cache_kidxt] * n_pages))

    tr = min(256, n)
    bias = pl.pallas_call(
        functools.partial(_select_kernel, k_sel=k_sel),
        grid=(n // tr,),
        in_specs=[pl.BlockSpec((tr, s_pad), lambda i: (i, 0))],
        out_specs=pl.BlockSpec((tr, s_pad), lambda i: (i, 0)),
        out_shape=jax.ShapeDtypeStruct((n, s_pad), F32),
        compiler_params=_cparams(("parallel",)),
        name="sample_select",
    )(score)

    mem_spec = pl.BlockSpec((None, None, N_MEM * N_HEADS_M, HEAD_DIM_M), lambda b, pt: (layer, b, 0, 0))
    return pl.pallas_call(
        functools.partial(_sattn_kernel, n_pages=n_pages, t_new=t_new),
        grid_spec=pltpu.PrefetchScalarGridSpec(
            num_scalar_prefetch=1, grid=(db,),
            in_specs=[brow(WIDTH_A), brow(s_pad), brow(WIDTH_A), brow(WIDTH_A),
                      pl.BlockSpec((None, N_HEADS_M * t_new, HEAD_DIM_M), lambda b, pt: (b, 0, 0)),
                      mem_spec, mem_spec]
                     + [page_spec(WIDTH_A, j) for j in range(n_pages)] * 2,
            out_specs=[brow(WIDTH_A), brow(WIDTH_M)]),
        out_shape=[jax.ShapeDtypeStruct((n, WIDTH_A), F32), jax.ShapeDtypeStruct((n, WIDTH_M), F32)],
        compiler_params=_cparams(("parallel",)),
        name="sample_attn",
    )(page_table, z["q"].astype(F32), bias, z["k"], z["v"], head_major(z["qm"], N_HEADS_M, HEAD_DIM_M),
      cache_mem_k2, cache_mem_v2, *([cache_kt] * n_pages), *([cache_vt] * n_pages))


def _lru_gates(xc, wa_ref, ba_ref, wx_ref, bx_ref, lam_ref):
    xb = xc.astype(BF16)
    r = jax.nn.sigmoid(_dot(xb, wa_ref[...]) + ba_ref[...])
    ig = jax.nn.sigmoid(_dot(xb, wx_ref[...]) + bx_ref[...])
    log_a = -LRU_C * r * _softplus(-lam_ref[...])
    a = jnp.exp(log_a)
    u = jnp.sqrt(-jnp.tanh(log_a) * (a * a + 1.0)) * ig * xc
    return a, u


def _rnn_prompt_kernel(xr_ref, gr_ref, cw_ref, cb_ref, wa_ref, ba_ref, wx_ref, bx_ref, lam_ref,
                       r_ref, h_ref, buf_ref, prev_s, hc_s, *, tt):
    @pl.when(pl.program_id(1) == 0)
    def _():
        prev_s[...] = jnp.zeros_like(prev_s)
        hc_s[...] = jnp.zeros_like(hc_s)

    x = xr_ref[...]
    ext = jnp.concatenate([prev_s[...], x], axis=0)
    conv = cb_ref[...] + cw_ref[RNN_CONV - 1:RNN_CONV] * ext
    for j in range(1, RNN_CONV):
        conv = conv + cw_ref[RNN_CONV - 1 - j:RNN_CONV - j] * pltpu.roll(ext, j, 0)
    xc = conv[SUBLANES:]
    a, u = _lru_gates(xc, wa_ref, ba_ref, wx_ref, bx_ref, lam_ref)

    row = lax.broadcasted_iota(jnp.int32, (tt, 1), 0)
    step = 1
    while step < tt:
        m = row >= step
        u = jnp.where(m, a * pltpu.roll(u, step, 0) + u, u)
        a = jnp.where(m, a * pltpu.roll(a, step, 0), a)
        step *= 2
    h = u + a * hc_s[0:1]
    r_ref[...] = (h * _gelu(gr_ref[...])).astype(BF16)
    h_tail = h[tt - SUBLANES:]
    h_ref[...] = h_tail
    hc_s[...] = jnp.broadcast_to(h_tail[SUBLANES - 1:SUBLANES], hc_s.shape)
    buf_ref[...] = x[tt - SUBLANES:]
    prev_s[...] = x[tt - SUBLANES:]


def _rnn_prompt(xr, gr, p, bsz, seq, tt=256):
    nt = seq // tt
    row = pl.BlockSpec((tt, WIDTH_RNN), lambda b, i: (b * nt + i, 0))
    tail = pl.BlockSpec((None, SUBLANES, WIDTH_RNN), lambda b, i: (b, 0, 0))
    vec = _const_spec((1, WIDTH_RNN))
    sq = _const_spec((WIDTH_RNN, WIDTH_RNN))
    return pl.pallas_call(
        functools.partial(_rnn_prompt_kernel, tt=tt),
        grid=(bsz, nt),
        in_specs=[row, row, _const_spec((RNN_CONV, WIDTH_RNN)), vec, sq, vec, sq, vec, vec],
        out_specs=[row, tail, tail],
        out_shape=[jax.ShapeDtypeStruct((bsz * seq, WIDTH_RNN), BF16),
                   jax.ShapeDtypeStruct((bsz, SUBLANES, WIDTH_RNN), F32),
                   jax.ShapeDtypeStruct((bsz, SUBLANES, WIDTH_RNN), F32)],
        scratch_shapes=[pltpu.VMEM((SUBLANES, WIDTH_RNN), F32), pltpu.VMEM((SUBLANES, WIDTH_RNN), F32)],
        compiler_params=_cparams(("parallel", "arbitrary")),
        name="rnn_prompt",
    )(xr, gr, p["rnn_conv_w"], p["rnn_conv_b"], p["rnn_wa"], p["rnn_ba"], p["rnn_wx"], p["rnn_bx"],
      p["rnn_lambda"])


def _rnn_sample_kernel(xr_ref, gr_ref, cbuf_ref, h0_ref, cw_ref, cb_ref, wa_ref, ba_ref, wx_ref, bx_ref,
                       lam_ref, r_ref, h_ref, buf_ref, *, db, t_new):
    x = xr_ref[...]
    n = db * t_new
    ext = jnp.concatenate([cbuf_ref[...], x], axis=0)
    conv = cb_ref[...] + cw_ref[0:1] * ext[0:n]
    for j in range(1, RNN_CONV):
        conv = conv + cw_ref[j:j + 1] * ext[j * db:j * db + n]
    a, u = _lru_gates(conv, wa_ref, ba_ref, wx_ref, bx_ref, lam_ref)
    h = h0_ref[...]
    hs = []
    for t in range(t_new):
        h = a[t * db:(t + 1) * db] * h + u[t * db:(t + 1) * db]
        hs.append(h)
    r_ref[...] = jnp.concatenate(hs, axis=0) * _gelu(gr_ref[...])
    h_ref[...] = h
    buf_ref[...] = ext[n:]


def _rnn_sample(xr_tm, gr_tm, cbuf_tm, h0, p, db, t_new):
    n = db * t_new
    nb = (RNN_CONV - 1) * db
    full = lambda r, c: _const_spec((r, c))
    vec = full(1, WIDTH_RNN)
    sq = full(WIDTH_RNN, WIDTH_RNN)
    return pl.pallas_call(
        functools.partial(_rnn_sample_kernel, db=db, t_new=t_new),
        grid=(1,),
        in_specs=[full(n, WIDTH_RNN), full(n, WIDTH_RNN), full(nb, WIDTH_RNN), full(db, WIDTH_RNN),
                  full(RNN_CONV, WIDTH_RNN), vec, sq, vec, sq, vec, vec],
        out_specs=[full(n, WIDTH_RNN), full(db, WIDTH_RNN), full(nb, WIDTH_RNN)],
        out_shape=[jax.ShapeDtypeStruct((n, WIDTH_RNN), F32), jax.ShapeDtypeStruct((db, WIDTH_RNN), F32),
                   jax.ShapeDtypeStruct((nb, WIDTH_RNN), F32)],
        compiler_params=_cparams(("arbitrary",)),
        name="rnn_sample",
    )(xr_tm, gr_tm, cbuf_tm, h0, p["rnn_conv_w"], p["rnn_conv_b"], p["rnn_wa"], p["rnn_ba"],
      p["rnn_wx"], p["rnn_bx"], p["rnn_lambda"])


def _merge_kernel(x_ref, a_ref, r_ref, m_ref, gate_ref, wb_ref, wo_ref, out_ref):
    y = None
    for n, br in enumerate((a_ref, r_ref, m_ref)):
        proj = _dot(br[...].astype(BF16), wb_ref[n])
        t = gate_ref[:, n * D_MODEL:(n + 1) * D_MODEL].astype(F32) * proj
        y = t if y is None else y + t
    out_ref[...] = x_ref[...] + _dot(y.astype(BF16), wo_ref[...])


def _merge(x, a, r, m, gate, p, tm):
    n = x.shape[0]
    row = lambda w: pl.BlockSpec((tm, w), lambda i: (i, 0))
    return pl.pallas_call(
        _merge_kernel,
        grid=(n // tm,),
        in_specs=[row(D_MODEL), row(WIDTH_A), row(WIDTH_RNN), row(WIDTH_M), row(N_BRANCH * D_MODEL),
                  _const_spec((N_BRANCH, WIDTH_A, D_MODEL)), _const_spec((D_MODEL, D_MODEL))],
        out_specs=row(D_MODEL),
        out_shape=jax.ShapeDtypeStruct((n, D_MODEL), F32),
        compiler_params=_cparams(("parallel",)),
        name="merge",
    )(x, a, r, m, gate, p["w_branch"], p["w_out"])


def _ffn_kernel(*refs, tm, stride, n_prev, use_state):
    if use_state:
        x_ref, g_ref, wg_ref, wv_ref, wd_ref, cw_ref, cb_ref, st_ref, out_ref, fb_ref, xn_s = refs
    else:
        x_ref, g_ref, wg_ref, wv_ref, wd_ref, cw_ref, cb_ref, out_ref, fb_ref, xn_s, carry_s = refs
    i = pl.program_id(1)
    j = pl.program_id(2)

    @pl.when(j == 0)
    def _():
        x = x_ref[...]
        xn_s[...] = _rmsnorm_rows(x, g_ref[...]).astype(BF16)
        out_ref[...] = x

    xn = xn_s[...]
    gt = _dot(xn, wg_ref[...])
    val = _dot(xn, wv_ref[...])
    if use_state:
        prev = st_ref[...]
    else:
        @pl.when(i == 0)
        def _():
            carry_s[j] = jnp.zeros((SUBLANES, gt.shape[1]), F32)

        prev = carry_s[j]
    ext = jnp.concatenate([prev, gt], axis=0)
    conv = cb_ref[...] + cw_ref[FFN_CONV - 1:FFN_CONV] * ext
    for s in range(1, FFN_CONV):
        conv = conv + cw_ref[FFN_CONV - 1 - s:FFN_CONV - s] * pltpu.roll(ext, s * stride, 0)
    act = _gelu(conv[n_prev:]) * val
    out_ref[...] += _dot(act.astype(BF16), wd_ref[...])
    fb_ref[...] = gt[tm - n_prev:]
    if not use_state:
        carry_s[j] = gt[tm - n_prev:]


def _ffn(x, p, *, groups, rows_per_group, tm, stride, state=None, tf=512):
    nt = rows_per_group // tm
    nj = D_FF // tf
    use_state = state is not None
    n_prev = (FFN_CONV - 1) * stride if use_state else SUBLANES
    n = groups * rows_per_group
    xrow = pl.BlockSpec((tm, D_MODEL), lambda b, i, j: (b * nt + i, 0))
    in_specs = [xrow, _const_spec((1, D_MODEL)),
                pl.BlockSpec((D_MODEL, tf), lambda b, i, j: (0, j)),
                pl.BlockSpec((D_MODEL, tf), lambda b, i, j: (0, nj + j)),
                pl.BlockSpec((tf, D_MODEL), lambda b, i, j: (j, 0)),
                pl.BlockSpec((FFN_CONV, tf), lambda b, i, j: (0, j)),
                pl.BlockSpec((1, tf), lambda b, i, j: (0, j))]
    args = [x, p["norm_ffn_g"], p["w_ffn_up"], p["w_ffn_up"], p["w_ffn_down"], p["ffn_conv_w"],
            p["ffn_conv_b"]]
    scratch = [pltpu.VMEM((tm, D_MODEL), BF16)]
    if use_state:
        in_specs.append(pl.BlockSpec((n_prev, tf), lambda b, i, j: (0, j)))
        args.append(state)
    else:
        scratch.append(pltpu.VMEM((nj, SUBLANES, tf), F32))
    return pl.pallas_call(
        functools.partial(_ffn_kernel, tm=tm, stride=stride, n_prev=n_prev, use_state=use_state),
        grid=(groups, nt, nj),
        in_specs=in_specs,
        out_specs=[xrow, pl.BlockSpec((None, n_prev, tf), lambda b, i, j: (b * nt + i, 0, j))],
        out_shape=[jax.ShapeDtypeStruct((n, D_MODEL), F32),
                   jax.ShapeDtypeStruct((groups * nt, n_prev, D_FF), F32)],
        scratch_shapes=scratch,
        compiler_params=_cparams(("parallel", "arbitrary", "arbitrary")),
        name="ffn",
    )(*args)


def _rope_tables(pos):
    half = ROPE_DIM // 2
    freqs = jnp.power(ROPE_THETA, -jnp.arange(half, dtype=F32) * 2.0 / ROPE_DIM)
    ang = pos.astype(F32)[:, None] * freqs[None, :]
    d = jnp.arange(LANES) % HEAD_DIM_A
    cos = jnp.cos(ang)[:, d % half]
    sin = jnp.sin(ang)[:, d % half]
    c = jnp.where(d < ROPE_DIM, cos, 1.0)
    s1 = jnp.where(d < half, -sin, 0.0)
    s2 = jnp.where((d >= half) & (d < ROPE_DIM), sin, 0.0)
    return c, s1, s2


def _block_diag(w):
    nl, nb, c, _ = w.shape
    eye = jnp.eye(nb, dtype=w.dtype)
    return (w[:, :, :, None, :] * eye[None, :, None, :, None]).reshape(nl, nb * c, nb * c)


def _prep_params(norm_mix_g, w_in, q_norm_g, k_norm_g, mq_norm_g, mk_norm_g, mem_norm_g, w_mem_kv,
                 rnn_conv_w, rnn_conv_b, rnn_wa, rnn_ba, rnn_wx, rnn_bx, rnn_lambda,
                 w_branch, w_out, norm_ffn_g, w_ffn_up, ffn_conv_w, ffn_conv_b, w_ffn_down):
    q, k, v, qi, ki, wi, xr, gr, qm, gates = jnp.split(w_in, IN_SPLITS, axis=-1)
    wi_p = jnp.pad(wi, ((0, 0), (0, 0), (0, LANES - N_IDX_HEADS)))
    w_in_p = jnp.concatenate([q, k, v, qi, ki, ki, wi_p, xr, gr, qm, gates], axis=-1).astype(BF16)
    row = lambda a: a[:, None, :]
    stacked = {
        "norm_mix_g": row(norm_mix_g), "w_in": w_in_p,
        "q_norm_g": row(jnp.tile(q_norm_g, (1, N_HEADS_A))), "k_norm_g": row(jnp.tile(k_norm_g, (1, N_HEADS_A))),
        "mq_norm_g": row(jnp.tile(mq_norm_g, (1, N_HEADS_M))), "mk_norm_g": row(jnp.tile(mk_norm_g, (1, N_HEADS_M))),
        "mem_norm_g": row(mem_norm_g), "w_mem_kv": w_mem_kv.astype(BF16),
        "rnn_conv_w": rnn_conv_w, "rnn_conv_b": row(rnn_conv_b),
        "rnn_wa": _block_diag(rnn_wa).astype(BF16), "rnn_ba": row(rnn_ba),
        "rnn_wx": _block_diag(rnn_wx).astype(BF16), "rnn_bx": row(rnn_bx),
        "rnn_lambda": row(rnn_lambda),
        "w_branch": w_branch.astype(BF16), "w_out": w_out.astype(BF16),
        "norm_ffn_g": row(norm_ffn_g), "w_ffn_up": w_ffn_up.astype(BF16),
        "ffn_conv_w": ffn_conv_w, "ffn_conv_b": row(ffn_conv_b), "w_ffn_down": w_ffn_down.astype(BF16),
    }
    ones = lambda n, c: jnp.kron(jnp.eye(n, dtype=F32), jnp.ones((c, c), F32)).astype(BF16)
    shared = {"bd64": ones(N_HEADS_A, HEAD_DIM_A), "bd128": ones(N_HEADS_M, HEAD_DIM_M)}
    return [dict({k_: v_[l] for k_, v_ in stacked.items()}, **shared) for l in range(DEPTH)]


def kernel(x_prompt, x_sample, mem_prompt, cache_k, cache_v, cache_kidx, page_table, cache_mem_k, cache_mem_v, state_rnn_h, state_rnn_conv, state_ffn_conv, norm_mix_g, w_in, q_norm_g, k_norm_g, mq_norm_g, mk_norm_g, mem_norm_g, w_mem_kv, rnn_conv_w, rnn_conv_b, rnn_wa, rnn_ba, rnn_wx, rnn_bx, rnn_lambda, w_branch, w_out, norm_ffn_g, w_ffn_up, ffn_conv_w, ffn_conv_b, w_ffn_down):
    bsz, seq, _ = x_prompt.shape
    db, t_new, _ = x_sample.shape
    n_pool = cache_k.shape[1]
    past = page_table.shape[1] * PAGE_SIZE
    params = _prep_params(norm_mix_g, w_in, q_norm_g, k_norm_g, mq_norm_g, mk_norm_g, mem_norm_g, w_mem_kv,
                          rnn_conv_w, rnn_conv_b, rnn_wa, rnn_ba, rnn_wx, rnn_bx, rnn_lambda,
                          w_branch, w_out, norm_ffn_g, w_ffn_up, ffn_conv_w, ffn_conv_b, w_ffn_down)
    tabs_p = _rope_tables(jnp.tile(jnp.arange(seq), bsz))
    tabs_s = _rope_tables(jnp.tile(past + jnp.arange(t_new), db))
    cache_kt = cache_k.transpose(0, 1, 3, 4, 2).reshape(DEPTH, n_pool, WIDTH_A, PAGE_SIZE)
    cache_vt = cache_v.transpose(0, 1, 3, 4, 2).reshape(DEPTH, n_pool, WIDTH_A, PAGE_SIZE)
    cache_kidxt = cache_kidx.transpose(0, 1, 3, 2)
    cache_mem_k2 = cache_mem_k.reshape(DEPTH, db, N_MEM * N_HEADS_M, HEAD_DIM_M)
    cache_mem_v2 = cache_mem_v.reshape(DEPTH, db, N_MEM * N_HEADS_M, HEAD_DIM_M)
    mem = mem_prompt.reshape(bsz * N_MEM, D_MODEL)

    def to_tm(a):
        return a.reshape(db, t_new, -1).transpose(1, 0, 2).reshape(db * t_new, -1)

    def to_bm(a):
        return a.reshape(t_new, db, -1).transpose(1, 0, 2).reshape(db * t_new, -1)

    xp = x_prompt.reshape(bsz * seq, D_MODEL)
    xs = x_sample.reshape(db * t_new, D_MODEL)
    outs = [[] for _ in range(14)]
    tm_p = min(512, seq)
    tm_s = min(512, db * t_new)
    for l in range(DEPTH):
        p = params[l]
        mk, mkb, mv, mvb = _memkv(mem, p)
        zp = _inproj(xp, p, tabs_p, tm=tm_p)
        a_p, m_p = _prompt_attention(zp, mkb, mvb, bsz, seq)
        r_p, h_p, rb_p = _rnn_prompt(zp["xr"], zp["gr"], p, bsz, seq)
        xp = _merge(xp, a_p, r_p, m_p, zp["gate"], p, tm=tm_p)
        xp, fb_p = _ffn(xp, p, groups=bsz, rows_per_group=seq, tm=tm_p, stride=1)
        zs = _inproj(xs, p, tabs_s, tm=tm_s)
        a_s, m_s = _sample_attention(zs, l, page_table, cache_kt, cache_vt, cache_kidxt,
                                     cache_mem_k2, cache_mem_v2, db, t_new)
        cbuf_tm = state_rnn_conv[l].transpose(1, 0, 2).reshape((RNN_CONV - 1) * db, WIDTH_RNN)
        r_s_tm, h_s, rb_s_tm = _rnn_sample(to_tm(zs["xr"]), to_tm(zs["gr"]), cbuf_tm, state_rnn_h[l],
                                            p, db, t_new)
        xs = _merge(xs, a_s, to_bm(r_s_tm), m_s, zs["gate"], p, tm=tm_s)
        fst_tm = state_ffn_conv[l].transpose(1, 0, 2).reshape((FFN_CONV - 1) * db, D_FF)
        xs_tm, fb_s_tm = _ffn(to_tm(xs), p, groups=1, rows_per_group=db * t_new, tm=db * t_new,
                              stride=db, state=fst_tm)
        xs = to_bm(xs_tm)

        new = (zp["k"].reshape(bsz, seq, N_HEADS_A, HEAD_DIM_A), zs["k"].reshape(db, t_new, N_HEADS_A, HEAD_DIM_A),
               zp["v"].reshape(bsz, seq, N_HEADS_A, HEAD_DIM_A), zs["v"].reshape(db, t_new, N_HEADS_A, HEAD_DIM_A),
               zp["ki"].reshape(bsz, seq, IDX_DIM), zs["ki"].reshape(db, t_new, IDX_DIM),
               mk.reshape(bsz, N_MEM, N_HEADS_M, HEAD_DIM_M), mv.reshape(bsz, N_MEM, N_HEADS_M, HEAD_DIM_M),
               h_p[:, SUBLANES - 1], h_s,
               rb_p[:, SUBLANES - (RNN_CONV - 1):],
               rb_s_tm.reshape(RNN_CONV - 1, db, WIDTH_RNN).transpose(1, 0, 2),
               fb_p.reshape(bsz, -1, SUBLANES, D_FF)[:, -1, SUBLANES - (FFN_CONV - 1):],
               fb_s_tm.reshape(FFN_CONV - 1, db, D_FF).transpose(1, 0, 2))
        for lst, val in zip(outs, new):
            lst.append(val)
    return (xp.reshape(bsz, seq, D_MODEL), xs.reshape(db, t_new, D_MODEL)) + tuple(jnp.stack(o) for o in outs)
```

```python
import functools

import jax
import jax.numpy as jnp
from jax import lax
from jax.experimental import pallas as pl
from jax.experimental.pallas import tpu as pltpu

F32 = jnp.float32
BF16 = jnp.bfloat16

D_MODEL = 1024
DEPTH = 4
PAGE_SIZE = 128
N_HEADS_A = 8
HEAD_DIM_A = 64
WIDTH_A = 512
N_IDX_HEADS = 8
IDX_DIM = 64
TOPK_MAX = 256
INDEX_WEIGHT_SCALE = N_IDX_HEADS ** -0.5 * IDX_DIM ** -0.5
WIDTH_RNN = 512
N_RNN_BLOCKS = 8
RNN_CONV = 4
LRU_C = 8.0
N_MEM = 256
N_HEADS_M = 4
HEAD_DIM_M = 128
WIDTH_M = 512
N_BRANCH = 3
D_FF = 3 * D_MODEL
FFN_CONV = 3
ROPE_THETA = 500000.0
ROPE_DIM = 16
EPS = 1e-6

IN_SIZES = (WIDTH_A, WIDTH_A, WIDTH_A, N_IDX_HEADS * IDX_DIM, IDX_DIM, N_IDX_HEADS,
            WIDTH_RNN, WIDTH_RNN, WIDTH_M, N_BRANCH * D_MODEL)
IN_SPLITS = tuple(sum(IN_SIZES[:i + 1]) for i in range(len(IN_SIZES) - 1))

LANES = 128
SUBLANES = 8
OFF_Q, OFF_K, OFF_V, OFF_QI = 0, 512, 1024, 1536
OFF_KI2 = 2048
OFF_WI = OFF_KI2 + LANES
OFF_XR = OFF_WI + LANES
OFF_GR = OFF_XR + WIDTH_RNN
OFF_QM = OFF_GR + WIDTH_RNN
OFF_G = OFF_QM + WIDTH_M
D_IN_P = OFF_G + N_BRANCH * D_MODEL

VMEM_LIMIT = 56 * 1024 * 1024

NEG_INF = float("-inf")
INT_MIN = -2 ** 31
NEG_INF_KEY = INT_MIN + 0x7FFFFF
MASKED = -1e30


def _cparams(sem):
    return pltpu.CompilerParams(dimension_semantics=sem, vmem_limit_bytes=VMEM_LIMIT)


def _const_spec(shape):
    nd = len(shape)
    return pl.BlockSpec(shape, lambda *_: (0,) * nd)


def _gelu(x):
    return 0.5 * x * (1.0 + jnp.tanh(0.7978845608028654 * (x + 0.044715 * (x * x * x))))


def _softplus(z):
    return jnp.maximum(z, 0.0) + jnp.log1p(jnp.exp(-jnp.abs(z)))


def _rmsnorm_rows(x, g):
    return x * lax.rsqrt(jnp.mean(x * x, axis=-1, keepdims=True) + EPS) * g


def _dot(a, b):
    return jnp.dot(a, b, preferred_element_type=F32)


def _dot_nt(a, b):
    return lax.dot_general(a, b, (((1,), (1,)), ((), ())), preferred_element_type=F32)


def _order_key(score):
    score = jnp.where(score == 0.0, 0.0, score)
    bits = lax.bitcast_convert_type(score, jnp.int32)
    return jnp.where(bits < 0, bits ^ jnp.int32(0x7FFFFFFF), bits)


def _select_bias(score, k_sel, bias_ref):
    rows, s_len = score.shape
    valid = score > NEG_INF
    score = jnp.where(score == 0.0, 0.0, score)
    bits = lax.bitcast_convert_type(score, jnp.int32)
    key = jnp.where(bits < 0, bits ^ jnp.int32(0x7FFFFFFF), bits)
    kf = jnp.float32(k_sel)

    def count_ge(c):
        return jnp.sum(jnp.where(key >= c, 1.0, 0.0), axis=1, keepdims=True)

    thr = jnp.where(count_ge(jnp.int32(0)) >= kf, jnp.int32(0), jnp.int32(INT_MIN))

    def body(it, thr):
        cand = thr + jnp.left_shift(jnp.int32(1), 30 - it)
        return jnp.where(count_ge(cand) >= kf, cand, thr)

    thr = lax.fori_loop(0, 31, body, thr)
    n_ge = count_ge(thr)
    bias_ref[...] = jnp.where((key >= thr) & valid, 0.0, NEG_INF)

    tie = jnp.where((n_ge > kf) & (thr > NEG_INF_KEY), 1.0, 0.0)

    @pl.when(jnp.max(tie) > 0.0)
    def _():
        gt = jnp.where((key > thr) & valid, 1.0, 0.0)
        eq = jnp.where((key == thr) & valid, 1.0, 0.0)
        need = kf - jnp.sum(gt, axis=1, keepdims=True)
        r_i = lax.broadcasted_iota(jnp.int32, (LANES, LANES), 0)
        c_i = lax.broadcasted_iota(jnp.int32, (LANES, LANES), 1)
        tri = jnp.where(r_i < c_i, 1.0, 0.0).astype(BF16)
        run = jnp.zeros((rows, 1), F32)
        for c in range(s_len // LANES):
            sl = slice(c * LANES, (c + 1) * LANES)
            eq_c = eq[:, sl]
            before = _dot(eq_c.astype(BF16), tri) + run
            sel = (gt[:, sl] > 0.0) | ((eq_c > 0.0) & (before < need))
            bias_ref[:, sl] = jnp.where(sel, 0.0, NEG_INF)
            run = run + jnp.sum(eq_c, axis=1, keepdims=True)


def _inproj_kernel(x_ref, g_ref, w_ref, bd64_ref, bd128_ref, qg_ref, kg_ref, mqg_ref,
                   c_ref, s1_ref, s2_ref,
                   q_ref, qx_ref, k_ref, kb_ref, v_ref, vx_ref, qi_ref, qix_ref, ki_ref, ki2_ref, wi_ref,
                   xr_ref, gr_ref, qm_ref, gate_ref):
    xn = _rmsnorm_rows(x_ref[...], g_ref[...]).astype(BF16)
    lo, hi = _half_masks()

    def expand_heads(z):
        parts = []
        for hp in range(z.shape[1] // LANES):
            pair = z[:, hp * LANES:(hp + 1) * LANES]
            parts += [jnp.where(lo, pair, 0.0), jnp.where(hi, pair, 0.0)]
        return jnp.concatenate(parts, axis=1).astype(BF16)

    def seg(off, width):
        return _dot(xn, w_ref[:, off:off + width])

    c1, s1, s2 = c_ref[...], s1_ref[...], s2_ref[...]

    def rope(z):
        w = z.shape[1]
        n = w // LANES
        cc, a, b = (jnp.tile(t, (1, n)) if n > 1 else t for t in (c1, s1, s2))
        return z * cc + pltpu.roll(z, w - ROPE_DIM // 2, 1) * a + pltpu.roll(z, ROPE_DIM // 2, 1) * b

    def headnorm(z, bd_ref, dim, g):
        ss = _dot((z * z).astype(BF16), bd_ref[...]) * (1.0 / dim)
        return z * lax.rsqrt(ss + EPS) * g

    q = rope(headnorm(seg(OFF_Q, WIDTH_A), bd64_ref, HEAD_DIM_A, qg_ref[...]))
    q = q * HEAD_DIM_A ** -0.5
    q_ref[...] = q.astype(BF16)
    qx_ref[...] = expand_heads(q)
    k = rope(headnorm(seg(OFF_K, WIDTH_A), bd64_ref, HEAD_DIM_A, kg_ref[...]))
    k_ref[...] = k
    kb_ref[...] = k.astype(BF16)
    v = seg(OFF_V, WIDTH_A)
    v_ref[...] = v
    vx_ref[...] = expand_heads(v)
    qi = rope(seg(OFF_QI, WIDTH_A))
    qi_ref[...] = qi.astype(BF16)
    qix_ref[...] = expand_heads(qi)
    ki2 = rope(seg(OFF_KI2, LANES))
    ki2_ref[...] = ki2.astype(BF16)
    ki_ref[...] = ki2[:, :IDX_DIM]
    wi_ref[...] = seg(OFF_WI, LANES) * INDEX_WEIGHT_SCALE
    xr_ref[...] = seg(OFF_XR, WIDTH_RNN)
    gr_ref[...] = seg(OFF_GR, WIDTH_RNN)
    qm_ref[...] = headnorm(seg(OFF_QM, WIDTH_M), bd128_ref, HEAD_DIM_M, mqg_ref[...]).astype(BF16)
    for c in range(N_BRANCH * D_MODEL // 512):
        gate_ref[:, c * 512:(c + 1) * 512] = jax.nn.sigmoid(seg(OFF_G + c * 512, 512)).astype(BF16)


def _inproj(x, p, rope_tabs, tm):
    n = x.shape[0]
    row = lambda w: pl.BlockSpec((tm, w), lambda i: (i, 0))
    outs = [("q", WIDTH_A, BF16), ("qx", 2 * WIDTH_A, BF16), ("k", WIDTH_A, F32), ("kb", WIDTH_A, BF16),
            ("v", WIDTH_A, F32), ("vx", 2 * WIDTH_A, BF16), ("qi", WIDTH_A, BF16),
            ("qix", 2 * WIDTH_A, BF16), ("ki", IDX_DIM, F32), ("ki2", LANES, BF16),
            ("wi", LANES, F32), ("xr", WIDTH_RNN, F32), ("gr", WIDTH_RNN, F32), ("qm", WIDTH_M, BF16),
            ("gate", N_BRANCH * D_MODEL, BF16)]
    res = pl.pallas_call(
        _inproj_kernel,
        grid=(n // tm,),
        in_specs=[row(D_MODEL), _const_spec((1, D_MODEL)),
                  pl.BlockSpec((D_MODEL, D_IN_P), lambda i: (0, 0), pipeline_mode=pl.Buffered(1)),
                  _const_spec((WIDTH_A, WIDTH_A)), _const_spec((WIDTH_M, WIDTH_M)),
                  _const_spec((1, WIDTH_A)), _const_spec((1, WIDTH_A)), _const_spec((1, WIDTH_M)),
                  row(LANES), row(LANES), row(LANES)],
        out_specs=[row(w) for _, w, _ in outs],
        out_shape=[jax.ShapeDtypeStruct((n, w), dt) for _, w, dt in outs],
        compiler_params=_cparams(("parallel",)),
        name="inproj",
    )(x, p["norm_mix_g"], p["w_in"], p["bd64"], p["bd128"], p["q_norm_g"], p["k_norm_g"],
      p["mq_norm_g"], *rope_tabs)
    return {name: r for (name, _, _), r in zip(outs, res)}


def _memkv_kernel(m_ref, g_ref, w_ref, bd128_ref, kg_ref, mk_ref, mkb_ref, mv_ref, mvb_ref):
    xn = _rmsnorm_rows(m_ref[...], g_ref[...]).astype(BF16)
    zk = _dot(xn, w_ref[:, :WIDTH_M])
    ss = _dot((zk * zk).astype(BF16), bd128_ref[...]) * (1.0 / HEAD_DIM_M)
    mk = zk * lax.rsqrt(ss + EPS) * kg_ref[...]
    mk_ref[...] = mk
    mkb_ref[...] = mk.astype(BF16)
    mv = _dot(xn, w_ref[:, WIDTH_M:])
    mv_ref[...] = mv
    mvb_ref[...] = mv.astype(BF16)


def _memkv(mem, p, tm=512):
    n = mem.shape[0]
    row = lambda w: pl.BlockSpec((tm, w), lambda i: (i, 0))
    return pl.pallas_call(
        _memkv_kernel,
        grid=(n // tm,),
        in_specs=[row(D_MODEL), _const_spec((1, D_MODEL)), _const_spec((D_MODEL, 2 * WIDTH_M)),
                  _const_spec((WIDTH_M, WIDTH_M)), _const_spec((1, WIDTH_M))],
        out_specs=[row(WIDTH_M)] * 4,
        out_shape=[jax.ShapeDtypeStruct((n, WIDTH_M), dt) for dt in (F32, BF16, F32, BF16)],
        compiler_params=_cparams(("parallel",)),
        name="memkv",
    )(mem, p["mem_norm_g"], p["w_mem_kv"], p["bd128"], p["mk_norm_g"])


def _half_masks():
    lane = lax.broadcasted_iota(jnp.int32, (1, LANES), 1)
    return lane < HEAD_DIM_A, lane >= HEAD_DIM_A


def _mem_attend(qm_ref, mk, mv, m_ref):
    for h in range(N_HEADS_M):
        sl = slice(h * HEAD_DIM_M, (h + 1) * HEAD_DIM_M)
        s = _dot_nt(qm_ref[:, sl], mk[:, sl]) * HEAD_DIM_M ** -0.5
        e = jnp.exp(s - jnp.max(s, axis=1, keepdims=True))
        o = _dot(e.astype(BF16), mv[:, sl])
        m_ref[:, sl] = (o * (1.0 / jnp.sum(e, axis=1, keepdims=True))).astype(m_ref.dtype)


def _pattn_kernel(qx_ref, qix_ref, wit_ref, qm_ref, kb_ref, vx_ref, ki2_ref, mk_ref, mv_ref,
                  a_ref, m_ref, key_s, bias_s, acc_s, m_s, l_s, *, tq, k_sel):
    i = pl.program_id(1)
    n_chunks = i + 1
    kf = jnp.float32(k_sel)
    n_sub = tq // LANES

    def rows(kc):
        return pl.ds(pl.multiple_of(kc * tq, tq), tq)

    qpos = i * tq + lax.broadcasted_iota(jnp.int32, (1, tq), 1)

    def index_chunk(kc, carry):
        ki2 = ki2_ref[rows(kc), :]
        score = jnp.zeros((tq, tq), F32)
        for h in range(N_IDX_HEADS):
            s = _dot_nt(ki2, qix_ref[:, h * LANES:(h + 1) * LANES])
            score = score + jnp.maximum(s, 0.0) * wit_ref[h:h + 1, :]
        kpos = kc * tq + lax.broadcasted_iota(jnp.int32, (tq, 1), 0)
        key_s[kc] = _order_key(jnp.where(kpos <= qpos, score, NEG_INF))
        return carry

    lax.fori_loop(0, n_chunks, index_chunk, 0)

    def count(pred):
        def body(kc, acc):
            hit = jnp.where(pred(key_s[kc]), 1.0, 0.0)
            return acc + jnp.sum(hit.reshape(tq // SUBLANES, SUBLANES, tq), axis=0)

        acc = lax.fori_loop(0, n_chunks, body, jnp.zeros((SUBLANES, tq), F32))
        return jnp.sum(acc, axis=0, keepdims=True)

    thr = jnp.where(count(lambda k: k >= 0) >= kf, jnp.int32(0), jnp.int32(INT_MIN))

    def bisect(it, thr):
        cand = thr + jnp.left_shift(jnp.int32(1), 30 - it)
        return jnp.where(count(lambda k: k >= cand) >= kf, cand, thr)

    thr = lax.fori_loop(0, 31, bisect, thr)
    n_ge = count(lambda k: k >= thr)

    def bias_chunk(kc, carry):
        k = key_s[kc]
        bias_s[kc] = jnp.where((k >= thr) & (k > NEG_INF_KEY), 0.0, MASKED).T
        return carry

    lax.fori_loop(0, n_chunks, bias_chunk, 0)

    tie = jnp.where((n_ge > kf) & (thr > NEG_INF_KEY), 1.0, 0.0)

    @pl.when(jnp.max(tie) > 0.0)
    def _():
        need = kf - count(lambda k: k > thr)
        r_i = lax.broadcasted_iota(jnp.int32, (LANES, LANES), 0)
        c_i = lax.broadcasted_iota(jnp.int32, (LANES, LANES), 1)
        tri = jnp.where(c_i < r_i, 1.0, 0.0).astype(BF16)

        def tie_chunk(kc, run):
            k = key_s[kc]
            for c in range(n_sub):
                sl = slice(c * LANES, (c + 1) * LANES)
                eq = jnp.where((k[sl] == thr) & (k[sl] > NEG_INF_KEY), 1.0, 0.0)
                before = _dot(tri, eq.astype(BF16)) + run
                sel = (k[sl] > thr) | ((eq > 0.0) & (before < need))
                bias_s[kc, :, sl] = jnp.where(sel, 0.0, MASKED).T
                run = run + jnp.sum(eq, axis=0, keepdims=True)
            return run

        lax.fori_loop(0, n_chunks, tie_chunk, jnp.zeros((1, tq), F32))


    acc_s[...] = jnp.zeros_like(acc_s)
    l_s[...] = jnp.zeros_like(l_s)
    m_s[...] = jnp.full(m_s.shape, MASKED, F32)

    def attend_chunk(kc, carry):
        bias = bias_s[kc]
        for h in range(N_HEADS_A):
            pair = slice((h // 2) * LANES, (h // 2 + 1) * LANES)
            hx = slice(h * LANES, (h + 1) * LANES)
            s = _dot_nt(qx_ref[:, hx], kb_ref[rows(kc), pair]) + bias
            m_old = m_s[h]
            m_new = jnp.maximum(m_old, jnp.max(s, axis=1, keepdims=True))
            alpha = jnp.exp(m_old - m_new)
            e = jnp.exp(s - jnp.tile(m_new, (1, n_sub)))
            l_s[h] = alpha * l_s[h] + jnp.sum(e, axis=1, keepdims=True)
            acc_s[:, hx] = alpha * acc_s[:, hx] + _dot(e.astype(BF16), vx_ref[rows(kc), hx])
            m_s[h] = m_new
        return carry

    lax.fori_loop(0, n_chunks, attend_chunk, 0)
    for hp in range(N_HEADS_A // 2):
        out = None
        for h in (2 * hp, 2 * hp + 1):
            t = acc_s[:, h * LANES:(h + 1) * LANES] * (1.0 / l_s[h])
            out = t if out is None else out + t
        a_ref[:, hp * LANES:(hp + 1) * LANES] = out.astype(BF16)

    _mem_attend(qm_ref, mk_ref[...], mv_ref[...], m_ref)


def _prompt_attention(z, mkb, mvb, bsz, seq, tq=256):
    nq = seq // tq
    k_sel = min(TOPK_MAX, seq // 4)
    qrow = lambda w: pl.BlockSpec((tq, w), lambda b, i: (b * nq + i, 0))
    brow = lambda rows, w: pl.BlockSpec((rows, w), lambda b, i: (b, 0))
    n = bsz * seq
    return pl.pallas_call(
        functools.partial(_pattn_kernel, tq=tq, k_sel=k_sel),
        grid=(bsz, nq),
        in_specs=[qrow(2 * WIDTH_A), qrow(2 * WIDTH_A),
                  pl.BlockSpec((N_IDX_HEADS, tq), lambda b, i: (0, b * nq + i)), qrow(WIDTH_M),
                  brow(seq, WIDTH_A), brow(seq, 2 * WIDTH_A), brow(seq, LANES),
                  brow(N_MEM, WIDTH_M), brow(N_MEM, WIDTH_M)],
        out_specs=[qrow(WIDTH_A), qrow(WIDTH_M)],
        out_shape=[jax.ShapeDtypeStruct((n, WIDTH_A), BF16), jax.ShapeDtypeStruct((n, WIDTH_M), BF16)],
        scratch_shapes=[pltpu.VMEM((nq, tq, tq), jnp.int32), pltpu.VMEM((nq, tq, tq), F32),
                        pltpu.VMEM((tq, N_HEADS_A * LANES), F32),
                        pltpu.VMEM((N_HEADS_A, tq, LANES), F32), pltpu.VMEM((N_HEADS_A, tq, LANES), F32)],
        compiler_params=_cparams(("parallel", "arbitrary")),
        name="prompt_attn",
    )(z["qx"], z["qix"], z["wi"][:, :N_IDX_HEADS].T, z["qm"], z["kb"], z["vx"], z["ki2"], mkb, mvb)


def _sidx_kernel(pt_ref, qb_ref, wi_ref, kin_ref, *rest, n_pages, t_new):
    page_refs, out_ref = rest[:n_pages], rest[n_pages]
    qb = qb_ref[...]
    wcol = jnp.concatenate([wi_ref[:, h:h + 1] for h in range(N_IDX_HEADS)], axis=0)

    def head_sum(s):
        r = jnp.maximum(s, 0.0) * wcol
        sc = r[0:t_new]
        for h in range(1, N_IDX_HEADS):
            sc = sc + r[h * t_new:(h + 1) * t_new]
        return sc

    for j in range(n_pages):
        out_ref[:, j * PAGE_SIZE:(j + 1) * PAGE_SIZE] = head_sum(_dot(qb, page_refs[j][...].astype(BF16)))
    knew = jnp.concatenate([kin_ref[...], jnp.zeros((PAGE_SIZE - t_new, IDX_DIM), F32)], axis=0)
    sc = head_sum(_dot_nt(qb, knew.astype(BF16)))
    tpos = lax.broadcasted_iota(jnp.int32, (t_new, PAGE_SIZE), 0)
    cpos = lax.broadcasted_iota(jnp.int32, (t_new, PAGE_SIZE), 1)
    out_ref[:, n_pages * PAGE_SIZE:] = jnp.where(cpos <= tpos, sc, NEG_INF)


def _select_kernel(score_ref, bias_ref, *, k_sel):
    _select_bias(score_ref[...], k_sel, bias_ref)


def _sattn_kernel(pt_ref, q_ref, bias_ref, kn_ref, vn_ref, qmb_ref, mk_ref, mv_ref, *rest,
                  n_pages, t_new):
    k_refs, v_refs = rest[:n_pages], rest[n_pages:2 * n_pages]
    a_ref, m_ref = rest[2 * n_pages:]

    rows = N_HEADS_A * t_new
    r_h = lax.broadcasted_iota(jnp.int32, (rows, WIDTH_A), 0) // t_new
    l_h = lax.broadcasted_iota(jnp.int32, (rows, WIDTH_A), 1) // HEAD_DIM_A
    keep = r_h == l_h
    qblk = jnp.where(keep, jnp.tile(q_ref[...], (N_HEADS_A, 1)), 0.0).astype(BF16)
    bias = jnp.tile(bias_ref[...], (N_HEADS_A, 1))
    pad = jnp.zeros((PAGE_SIZE - t_new, WIDTH_A), F32)
    knew = jnp.concatenate([kn_ref[...], pad], axis=0).astype(BF16)
    vnew = jnp.concatenate([vn_ref[...], pad], axis=0).astype(BF16)
    chunks = [_dot(qblk, k_refs[j][...].astype(BF16)) for j in range(n_pages)]
    chunks.append(_dot_nt(qblk, knew))
    s = jnp.concatenate(chunks, axis=1) + bias
    e = jnp.exp(s - jnp.max(s, axis=1, keepdims=True))
    linv = 1.0 / jnp.sum(e, axis=1, keepdims=True)
    eb = e.astype(BF16)
    o = _dot(eb[:, n_pages * PAGE_SIZE:], vnew)
    for j in range(n_pages):
        o = o + _dot_nt(eb[:, j * PAGE_SIZE:(j + 1) * PAGE_SIZE], v_refs[j][...].astype(BF16))
    o = jnp.where(keep, o * linv, 0.0)
    out = o[0:t_new]
    for h in range(1, N_HEADS_A):
        out = out + o[h * t_new:(h + 1) * t_new]
    a_ref[...] = out

    rows_m = N_HEADS_M * t_new
    n_col = N_MEM * N_HEADS_M
    row_h = lax.broadcasted_iota(jnp.int32, (rows_m, n_col), 0) // t_new
    col_h = lax.broadcasted_iota(jnp.int32, (rows_m, n_col), 1) % N_HEADS_M
    sm = _dot_nt(qmb_ref[...], mk_ref[...].astype(BF16)) * HEAD_DIM_M ** -0.5
    sm = jnp.where(row_h == col_h, sm, NEG_INF)
    em = jnp.exp(sm - jnp.max(sm, axis=1, keepdims=True))
    om = _dot(em.astype(BF16), mv_ref[...].astype(BF16)) * (1.0 / jnp.sum(em, axis=1, keepdims=True))
    m_ref[...] = jnp.concatenate([om[h * t_new:(h + 1) * t_new] for h in range(N_HEADS_M)], axis=1)


def _sample_attention(z, layer, page_table, cache_kt, cache_vt, cache_kidxt, cache_mem_k2, cache_mem_v2,
                      db, t_new):
    n_pages = page_table.shape[1]
    past = n_pages * PAGE_SIZE
    s_pad = past + PAGE_SIZE
    k_sel = min(TOPK_MAX, (past + t_new) // 4)
    n = db * t_new

    def head_major(a, n_heads, head_dim):
        return a.reshape(db, t_new, n_heads, head_dim).transpose(0, 2, 1, 3).reshape(
            db, n_heads * t_new, head_dim)

    brow = lambda w: pl.BlockSpec((t_new, w), lambda b, pt: (b, 0))

    def page_spec(rows, j):
        return pl.BlockSpec((None, None, rows, PAGE_SIZE), lambda b, pt: (layer, pt[b, j], 0, 0))

    score = pl.pallas_call(
        functools.partial(_sidx_kernel, n_pages=n_pages, t_new=t_new),
        grid_spec=pltpu.PrefetchScalarGridSpec(
            num_scalar_prefetch=1, grid=(db,),
            in_specs=[pl.BlockSpec((None, N_IDX_HEADS * t_new, IDX_DIM), lambda b, pt: (b, 0, 0)),
                      brow(LANES), brow(IDX_DIM)] + [page_spec(IDX_DIM, j) for j in range(n_pages)],
            out_specs=brow(s_pad)),
        out_shape=jax.ShapeDtypeStruct((n, s_pad), F32),
        compiler_params=_cparams(("parallel",)),
        name="sample_index",
    )(page_table, head_major(z["qi"], N_IDX_HEADS, IDX_DIM), z["wi"], z["ki"], *([cache_kidxt] * n_pages))

    tr = min(256, n)
    bias = pl.pallas_call(
        functools.partial(_select_kernel, k_sel=k_sel),
        grid=(n // tr,),
        in_specs=[pl.BlockSpec((tr, s_pad), lambda i: (i, 0))],
        out_specs=pl.BlockSpec((tr, s_pad), lambda i: (i, 0)),
        out_shape=jax.ShapeDtypeStruct((n, s_pad), F32),
        compiler_params=_cparams(("parallel",)),
        name="sample_select",
    )(score)

    mem_spec = pl.BlockSpec((None, None, N_MEM * N_HEADS_M, HEAD_DIM_M), lambda b, pt: (layer, b, 0, 0))
    return pl.pallas_call(
        functools.partial(_sattn_kernel, n_pages=n_pages, t_new=t_new),
        grid_spec=pltpu.PrefetchScalarGridSpec(
            num_scalar_prefetch=1, grid=(db,),
            in_specs=[brow(WIDTH_A), brow(s_pad), brow(WIDTH_A), brow(WIDTH_A),
                      pl.BlockSpec((None, N_HEADS_M * t_new, HEAD_DIM_M), lambda b, pt: (b, 0, 0)),
                      mem_spec, mem_spec]
                     + [page_spec(WIDTH_A, j) for j in range(n_pages)] * 2,
            out_specs=[brow(WIDTH_A), brow(WIDTH_M)]),
        out_shape=[jax.ShapeDtypeStruct((n, WIDTH_A), F32), jax.ShapeDtypeStruct((n, WIDTH_M), F32)],
        compiler_params=_cparams(("parallel",)),
        name="sample_attn",
    )(page_table, z["q"].astype(F32), bias, z["k"], z["v"], head_major(z["qm"], N_HEADS_M, HEAD_DIM_M),
      cache_mem_k2, cache_mem_v2, *([cache_kt] * n_pages), *([cache_vt] * n_pages))


def _lru_gates(xc, wa_ref, ba_ref, wx_ref, bx_ref, lam_ref):
    xb = xc.astype(BF16)
    r = jax.nn.sigmoid(_dot(xb, wa_ref[...]) + ba_ref[...])
    ig = jax.nn.sigmoid(_dot(xb, wx_ref[...]) + bx_ref[...])
    log_a = -LRU_C * r * _softplus(-lam_ref[...])
    a = jnp.exp(log_a)
    u = jnp.sqrt(-jnp.tanh(log_a) * (a * a + 1.0)) * ig * xc
    return a, u


def _rnn_prompt_kernel(xr_ref, gr_ref, cw_ref, cb_ref, wa_ref, ba_ref, wx_ref, bx_ref, lam_ref,
                       r_ref, h_ref, buf_ref, prev_s, hc_s, *, tt):
    @pl.when(pl.program_id(1) == 0)
    def _():
        prev_s[...] = jnp.zeros_like(prev_s)
        hc_s[...] = jnp.zeros_like(hc_s)

    x = xr_ref[...]
    ext = jnp.concatenate([prev_s[...], x], axis=0)
    conv = cb_ref[...] + cw_ref[RNN_CONV - 1:RNN_CONV] * ext
    for j in range(1, RNN_CONV):
        conv = conv + cw_ref[RNN_CONV - 1 - j:RNN_CONV - j] * pltpu.roll(ext, j, 0)
    xc = conv[SUBLANES:]
    a, u = _lru_gates(xc, wa_ref, ba_ref, wx_ref, bx_ref, lam_ref)

    row = lax.broadcasted_iota(jnp.int32, (tt, 1), 0)
    step = 1
    while step < tt:
        m = row >= step
        u = jnp.where(m, a * pltpu.roll(u, step, 0) + u, u)
        a = jnp.where(m, a * pltpu.roll(a, step, 0), a)
        step *= 2
    h = u + a * hc_s[0:1]
    r_ref[...] = (h * _gelu(gr_ref[...])).astype(BF16)
    h_tail = h[tt - SUBLANES:]
    h_ref[...] = h_tail
    hc_s[...] = jnp.broadcast_to(h_tail[SUBLANES - 1:SUBLANES], hc_s.shape)
    buf_ref[...] = x[tt - SUBLANES:]
    prev_s[...] = x[tt - SUBLANES:]


def _rnn_prompt(xr, gr, p, bsz, seq, tt=256):
    nt = seq // tt
    row = pl.BlockSpec((tt, WIDTH_RNN), lambda b, i: (b * nt + i, 0))
    tail = pl.BlockSpec((None, SUBLANES, WIDTH_RNN), lambda b, i: (b, 0, 0))
    vec = _const_spec((1, WIDTH_RNN))
    sq = _const_spec((WIDTH_RNN, WIDTH_RNN))
    return pl.pallas_call(
        functools.partial(_rnn_prompt_kernel, tt=tt),
        grid=(bsz, nt),
        in_specs=[row, row, _const_spec((RNN_CONV, WIDTH_RNN)), vec, sq, vec, sq, vec, vec],
        out_specs=[row, tail, tail],
        out_shape=[jax.ShapeDtypeStruct((bsz * seq, WIDTH_RNN), BF16),
                   jax.ShapeDtypeStruct((bsz, SUBLANES, WIDTH_RNN), F32),
                   jax.ShapeDtypeStruct((bsz, SUBLANES, WIDTH_RNN), F32)],
        scratch_shapes=[pltpu.VMEM((SUBLANES, WIDTH_RNN), F32), pltpu.VMEM((SUBLANES, WIDTH_RNN), F32)],
        compiler_params=_cparams(("parallel", "arbitrary")),
        name="rnn_prompt",
    )(xr, gr, p["rnn_conv_w"], p["rnn_conv_b"], p["rnn_wa"], p["rnn_ba"], p["rnn_wx"], p["rnn_bx"],
      p["rnn_lambda"])


def _rnn_sample_kernel(xr_ref, gr_ref, cbuf_ref, h0_ref, cw_ref, cb_ref, wa_ref, ba_ref, wx_ref, bx_ref,
                       lam_ref, r_ref, h_ref, buf_ref, *, db, t_new):
    x = xr_ref[...]
    n = db * t_new
    ext = jnp.concatenate([cbuf_ref[...], x], axis=0)
    conv = cb_ref[...] + cw_ref[0:1] * ext[0:n]
    for j in range(1, RNN_CONV):
        conv = conv + cw_ref[j:j + 1] * ext[j * db:j * db + n]
    a, u = _lru_gates(conv, wa_ref, ba_ref, wx_ref, bx_ref, lam_ref)
    h = h0_ref[...]
    hs = []
    for t in range(t_new):
        h = a[t * db:(t + 1) * db] * h + u[t * db:(t + 1) * db]
        hs.append(h)
    r_ref[...] = jnp.concatenate(hs, axis=0) * _gelu(gr_ref[...])
    h_ref[...] = h
    buf_ref[...] = ext[n:]


def _rnn_sample(xr_tm, gr_tm, cbuf_tm, h0, p, db, t_new):
    n = db * t_new
    nb = (RNN_CONV - 1) * db
    full = lambda r, c: _const_spec((r, c))
    vec = full(1, WIDTH_RNN)
    sq = full(WIDTH_RNN, WIDTH_RNN)
    return pl.pallas_call(
        functools.partial(_rnn_sample_kernel, db=db, t_new=t_new),
        grid=(1,),
        in_specs=[full(n, WIDTH_RNN), full(n, WIDTH_RNN), full(nb, WIDTH_RNN), full(db, WIDTH_RNN),
                  full(RNN_CONV, WIDTH_RNN), vec, sq, vec, sq, vec, vec],
        out_specs=[full(n, WIDTH_RNN), full(db, WIDTH_RNN), full(nb, WIDTH_RNN)],
        out_shape=[jax.ShapeDtypeStruct((n, WIDTH_RNN), F32), jax.ShapeDtypeStruct((db, WIDTH_RNN), F32),
                   jax.ShapeDtypeStruct((nb, WIDTH_RNN), F32)],
        compiler_params=_cparams(("arbitrary",)),
        name="rnn_sample",
    )(xr_tm, gr_tm, cbuf_tm, h0, p["rnn_conv_w"], p["rnn_conv_b"], p["rnn_wa"], p["rnn_ba"],
      p["rnn_wx"], p["rnn_bx"], p["rnn_lambda"])


def _merge_kernel(x_ref, a_ref, r_ref, m_ref, gate_ref, wb_ref, wo_ref, out_ref):
    y = None
    for n, br in enumerate((a_ref, r_ref, m_ref)):
        proj = _dot(br[...].astype(BF16), wb_ref[n])
        t = gate_ref[:, n * D_MODEL:(n + 1) * D_MODEL].astype(F32) * proj
        y = t if y is None else y + t
    out_ref[...] = x_ref[...] + _dot(y.astype(BF16), wo_ref[...])


def _merge(x, a, r, m, gate, p, tm):
    n = x.shape[0]
    row = lambda w: pl.BlockSpec((tm, w), lambda i: (i, 0))
    return pl.pallas_call(
        _merge_kernel,
        grid=(n // tm,),
        in_specs=[row(D_MODEL), row(WIDTH_A), row(WIDTH_RNN), row(WIDTH_M), row(N_BRANCH * D_MODEL),
                  _const_spec((N_BRANCH, WIDTH_A, D_MODEL)), _const_spec((D_MODEL, D_MODEL))],
        out_specs=row(D_MODEL),
        out_shape=jax.ShapeDtypeStruct((n, D_MODEL), F32),
        compiler_params=_cparams(("parallel",)),
        name="merge",
    )(x, a, r, m, gate, p["w_branch"], p["w_out"])


def _ffn_kernel(*refs, tm, stride, n_prev, use_state):
    if use_state:
        x_ref, g_ref, wg_ref, wv_ref, wd_ref, cw_ref, cb_ref, st_ref, out_ref, fb_ref, xn_s = refs
    else:
        x_ref, g_ref, wg_ref, wv_ref, wd_ref, cw_ref, cb_ref, out_ref, fb_ref, xn_s, carry_s = refs
    i = pl.program_id(1)
    j = pl.program_id(2)

    @pl.when(j == 0)
    def _():
        x = x_ref[...]
        xn_s[...] = _rmsnorm_rows(x, g_ref[...]).astype(BF16)
        out_ref[...] = x

    xn = xn_s[...]
    gt = _dot(xn, wg_ref[...])
    val = _dot(xn, wv_ref[...])
    if use_state:
        prev = st_ref[...]
    else:
        @pl.when(i == 0)
        def _():
            carry_s[j] = jnp.zeros((SUBLANES, gt.shape[1]), F32)

        prev = carry_s[j]
    ext = jnp.concatenate([prev, gt], axis=0)
    conv = cb_ref[...] + cw_ref[FFN_CONV - 1:FFN_CONV] * ext
    for s in range(1, FFN_CONV):
        conv = conv + cw_ref[FFN_CONV - 1 - s:FFN_CONV - s] * pltpu.roll(ext, s * stride, 0)
    act = _gelu(conv[n_prev:]) * val
    out_ref[...] += _dot(act.astype(BF16), wd_ref[...])
    fb_ref[...] = gt[tm - n_prev:]
    if not use_state:
        carry_s[j] = gt[tm - n_prev:]


def _ffn(x, p, *, groups, rows_per_group, tm, stride, state=None, tf=512):
    nt = rows_per_group // tm
    nj = D_FF // tf
    use_state = state is not None
    n_prev = (FFN_CONV - 1) * stride if use_state else SUBLANES
    n = groups * rows_per_group
    xrow = pl.BlockSpec((tm, D_MODEL), lambda b, i, j: (b * nt + i, 0))
    in_specs = [xrow, _const_spec((1, D_MODEL)),
                pl.BlockSpec((D_MODEL, tf), lambda b, i, j: (0, j)),
                pl.BlockSpec((D_MODEL, tf), lambda b, i, j: (0, nj + j)),
                pl.BlockSpec((tf, D_MODEL), lambda b, i, j: (j, 0)),
                pl.BlockSpec((FFN_CONV, tf), lambda b, i, j: (0, j)),
                pl.BlockSpec((1, tf), lambda b, i, j: (0, j))]
    args = [x, p["norm_ffn_g"], p["w_ffn_up"], p["w_ffn_up"], p["w_ffn_down"], p["ffn_conv_w"],
            p["ffn_conv_b"]]
    scratch = [pltpu.VMEM((tm, D_MODEL), BF16)]
    if use_state:
        in_specs.append(pl.BlockSpec((n_prev, tf), lambda b, i, j: (0, j)))
        args.append(state)
    else:
        scratch.append(pltpu.VMEM((nj, SUBLANES, tf), F32))
    return pl.pallas_call(
        functools.partial(_ffn_kernel, tm=tm, stride=stride, n_prev=n_prev, use_state=use_state),
        grid=(groups, nt, nj),
        in_specs=in_specs,
        out_specs=[xrow, pl.BlockSpec((None, n_prev, tf), lambda b, i, j: (b * nt + i, 0, j))],
        out_shape=[jax.ShapeDtypeStruct((n, D_MODEL), F32),
                   jax.ShapeDtypeStruct((groups * nt, n_prev, D_FF), F32)],
        scratch_shapes=scratch,
        compiler_params=_cparams(("parallel", "arbitrary", "arbitrary")),
        name="ffn",
    )(*args)


def _rope_tables(pos):
    half = ROPE_DIM // 2
    freqs = jnp.power(ROPE_THETA, -jnp.arange(half, dtype=F32) * 2.0 / ROPE_DIM)
    ang = pos.astype(F32)[:, None] * freqs[None, :]
    d = jnp.arange(LANES) % HEAD_DIM_A
    cos = jnp.cos(ang)[:, d % half]
    sin = jnp.sin(ang)[:, d % half]
    c = jnp.where(d < ROPE_DIM, cos, 1.0)
    s1 = jnp.where(d < half, -sin, 0.0)
    s2 = jnp.where((d >= half) & (d < ROPE_DIM), sin, 0.0)
    return c, s1, s2


def _block_diag(w):
    nl, nb, c, _ = w.shape
    eye = jnp.eye(nb, dtype=w.dtype)
    return (w[:, :, :, None, :] * eye[None, :, None, :, None]).reshape(nl, nb * c, nb * c)


def _prep_params(norm_mix_g, w_in, q_norm_g, k_norm_g, mq_norm_g, mk_norm_g, mem_norm_g, w_mem_kv,
                 rnn_conv_w, rnn_conv_b, rnn_wa, rnn_ba, rnn_wx, rnn_bx, rnn_lambda,
                 w_branch, w_out, norm_ffn_g, w_ffn_up, ffn_conv_w, ffn_conv_b, w_ffn_down):
    q, k, v, qi, ki, wi, xr, gr, qm, gates = jnp.split(w_in, IN_SPLITS, axis=-1)
    wi_p = jnp.pad(wi, ((0, 0), (0, 0), (0, LANES - N_IDX_HEADS)))
    w_in_p = jnp.concatenate([q, k, v, qi, ki, ki, wi_p, xr, gr, qm, gates], axis=-1).astype(BF16)
    row = lambda a: a[:, None, :]
    stacked = {
        "norm_mix_g": row(norm_mix_g), "w_in": w_in_p,
        "q_norm_g": row(jnp.tile(q_norm_g, (1, N_HEADS_A))), "k_norm_g": row(jnp.tile(k_norm_g, (1, N_HEADS_A))),
        "mq_norm_g": row(jnp.tile(mq_norm_g, (1, N_HEADS_M))), "mk_norm_g": row(jnp.tile(mk_norm_g, (1, N_HEADS_M))),
        "mem_norm_g": row(mem_norm_g), "w_mem_kv": w_mem_kv.astype(BF16),
        "rnn_conv_w": rnn_conv_w, "rnn_conv_b": row(rnn_conv_b),
        "rnn_wa": _block_diag(rnn_wa).astype(BF16), "rnn_ba": row(rnn_ba),
        "rnn_wx": _block_diag(rnn_wx).astype(BF16), "rnn_bx": row(rnn_bx),
        "rnn_lambda": row(rnn_lambda),
        "w_branch": w_branch.astype(BF16), "w_out": w_out.astype(BF16),
        "norm_ffn_g": row(norm_ffn_g), "w_ffn_up": w_ffn_up.astype(BF16),
        "ffn_conv_w": ffn_conv_w, "ffn_conv_b": row(ffn_conv_b), "w_ffn_down": w_ffn_down.astype(BF16),
    }
    ones = lambda n, c: jnp.kron(jnp.eye(n, dtype=F32), jnp.ones((c, c), F32)).astype(BF16)
    shared = {"bd64": ones(N_HEADS_A, HEAD_DIM_A), "bd128": ones(N_HEADS_M, HEAD_DIM_M)}
    return [dict({k_: v_[l] for k_, v_ in stacked.items()}, **shared) for l in range(DEPTH)]


def kernel(x_prompt, x_sample, mem_prompt, cache_k, cache_v, cache_kidx, page_table, cache_mem_k, cache_mem_v, state_rnn_h, state_rnn_conv, state_ffn_conv, norm_mix_g, w_in, q_norm_g, k_norm_g, mq_norm_g, mk_norm_g, mem_norm_g, w_mem_kv, rnn_conv_w, rnn_conv_b, rnn_wa, rnn_ba, rnn_wx, rnn_bx, rnn_lambda, w_branch, w_out, norm_ffn_g, w_ffn_up, ffn_conv_w, ffn_conv_b, w_ffn_down):
    bsz, seq, _ = x_prompt.shape
    db, t_new, _ = x_sample.shape
    n_pool = cache_k.shape[1]
    past = page_table.shape[1] * PAGE_SIZE
    params = _prep_params(norm_mix_g, w_in, q_norm_g, k_norm_g, mq_norm_g, mk_norm_g, mem_norm_g, w_mem_kv,
                          rnn_conv_w, rnn_conv_b, rnn_wa, rnn_ba, rnn_wx, rnn_bx, rnn_lambda,
                          w_branch, w_out, norm_ffn_g, w_ffn_up, ffn_conv_w, ffn_conv_b, w_ffn_down)
    tabs_p = _rope_tables(jnp.tile(jnp.arange(seq), bsz))
    tabs_s = _rope_tables(jnp.tile(past + jnp.arange(t_new), db))
    cache_kt = cache_k.transpose(0, 1, 3, 4, 2).reshape(DEPTH, n_pool, WIDTH_A, PAGE_SIZE)
    cache_vt = cache_v.transpose(0, 1, 3, 4, 2).reshape(DEPTH, n_pool, WIDTH_A, PAGE_SIZE)
    cache_kidxt = cache_kidx.transpose(0, 1, 3, 2)
    cache_mem_k2 = cache_mem_k.reshape(DEPTH, db, N_MEM * N_HEADS_M, HEAD_DIM_M)
    cache_mem_v2 = cache_mem_v.reshape(DEPTH, db, N_MEM * N_HEADS_M, HEAD_DIM_M)
    mem = mem_prompt.reshape(bsz * N_MEM, D_MODEL)

    def to_tm(a):
        return a.reshape(db, t_new, -1).transpose(1, 0, 2).reshape(db * t_new, -1)

    def to_bm(a):
        return a.reshape(t_new, db, -1).transpose(1, 0, 2).reshape(db * t_new, -1)

    xp = x_prompt.reshape(bsz * seq, D_MODEL)
    xs = x_sample.reshape(db * t_new, D_MODEL)
    outs = [[] for _ in range(14)]
    tm_p = min(512, seq)
    tm_f = min(1024, seq)
    tm_s = min(512, db * t_new)
    for l in range(DEPTH):
        p = params[l]
        mk, mkb, mv, mvb = _memkv(mem, p)
        zp = _inproj(xp, p, tabs_p, tm=tm_p)
        a_p, m_p = _prompt_attention(zp, mkb, mvb, bsz, seq)
        r_p, h_p, rb_p = _rnn_prompt(zp["xr"], zp["gr"], p, bsz, seq)
        xp = _merge(xp, a_p, r_p, m_p, zp["gate"], p, tm=tm_p)
        xp, fb_p = _ffn(xp, p, groups=bsz, rows_per_group=seq, tm=tm_f, stride=1)
        zs = _inproj(xs, p, tabs_s, tm=tm_s)
        a_s, m_s = _sample_attention(zs, l, page_table, cache_kt, cache_vt, cache_kidxt,
                                     cache_mem_k2, cache_mem_v2, db, t_new)
        cbuf_tm = state_rnn_conv[l].transpose(1, 0, 2).reshape((RNN_CONV - 1) * db, WIDTH_RNN)
        r_s_tm, h_s, rb_s_tm = _rnn_sample(to_tm(zs["xr"]), to_tm(zs["gr"]), cbuf_tm, state_rnn_h[l],
                                            p, db, t_new)
        xs = _merge(xs, a_s, to_bm(r_s_tm), m_s, zs["gate"], p, tm=tm_s)
        fst_tm = state_ffn_conv[l].transpose(1, 0, 2).reshape((FFN_CONV - 1) * db, D_FF)
        xs_tm, fb_s_tm = _ffn(to_tm(xs), p, groups=1, rows_per_group=db * t_new, tm=db * t_new,
                              stride=db, state=fst_tm)
        xs = to_bm(xs_tm)

        new = (zp["k"].reshape(bsz, seq, N_HEADS_A, HEAD_DIM_A), zs["k"].reshape(db, t_new, N_HEADS_A, HEAD_DIM_A),
               zp["v"].reshape(bsz, seq, N_HEADS_A, HEAD_DIM_A), zs["v"].reshape(db, t_new, N_HEADS_A, HEAD_DIM_A),
               zp["ki"].reshape(bsz, seq, IDX_DIM), zs["ki"].reshape(db, t_new, IDX_DIM),
               mk.reshape(bsz, N_MEM, N_HEADS_M, HEAD_DIM_M), mv.reshape(bsz, N_MEM, N_HEADS_M, HEAD_DIM_M),
               h_p[:, SUBLANES - 1], h_s,
               rb_p[:, SUBLANES - (RNN_CONV - 1):],
               rb_s_tm.reshape(RNN_CONV - 1, db, WIDTH_RNN).transpose(1, 0, 2),
               fb_p.reshape(bsz, -1, SUBLANES, D_FF)[:, -1, SUBLANES - (FFN_CONV - 1):],
               fb_s_tm.reshape(FFN_CONV - 1, db, D_FF).transpose(1, 0, 2))
        for lst, val in zip(outs, new):
            lst.append(val)
    return (xp.reshape(bsz, seq, D_MODEL), xs.reshape(db, t_new, D_MODEL)) + tuple(jnp.stack(o) for o in outs)
```
